```python
import jax, jax.numpy as jnp
from jax import lax
import numpy as np

D_MODEL = 1024
BATCH = 8
SEQ = 4096
DEPTH = 1
DEC_BATCH = 8
DEC_SEQ = 64
PAST_LEN = 1024

CHUNK = 64
N_LEFT_CHUNKS = 8
LEFT_CTX = N_LEFT_CHUNKS * CHUNK
ATT_HEADS = 8
HEAD_DIM = 64
ATT_DIM = ATT_HEADS * HEAD_DIM
REL_CLIP = 128
N_REL = 2 * REL_CLIP + 1
CONV_DIM = D_MODEL // 2
CONV_WIDTH = 3
PEER_HEADS = 8
PEER_NKEYS = 128
PEER_N = PEER_NKEYS * PEER_NKEYS
PEER_DKEY = 256
PEER_HALF = PEER_DKEY // 2
PEER_TOPK = 16
PEER_BLOCK = 128
EPS = 1e-6
IN_COLS = 3 * ATT_DIM + 3 * CONV_DIM + 2 * D_MODEL
IN_SPLITS = (ATT_DIM, 2 * ATT_DIM, 3 * ATT_DIM,
             3 * ATT_DIM + CONV_DIM, 3 * ATT_DIM + 2 * CONV_DIM, 3 * ATT_DIM + 3 * CONV_DIM,
             3 * ATT_DIM + 3 * CONV_DIM + D_MODEL)

kernel_name = "hybrid_stream_bandattn_shortconv_peer_step"


def rms_norm(x, gain):
    xf = x.astype(jnp.float32)
    y = xf * lax.rsqrt(jnp.mean(xf * xf, axis=-1, keepdims=True) + EPS)
    return (y * gain.astype(jnp.float32)).astype(x.dtype)


def adaln(c, w_ada, b_ada):
    mod = jax.nn.silu(c) @ w_ada + b_ada
    return jnp.split(mod[:, None, :], 6, axis=-1)


def rel_bias(rel_table, n_q, n_k, offset):
    dist = jnp.arange(n_q)[:, None] + offset - jnp.arange(n_k)[None, :]
    idx = jnp.clip(dist, -REL_CLIP, REL_CLIP) + REL_CLIP
    return rel_table[:, idx]


def band_softmax_attend(q, k, v, bias, valid):
    s = jnp.einsum("bqhd,bkhd->bhqk", q, k).astype(jnp.float32) * (HEAD_DIM ** -0.5)
    s = s + bias.astype(jnp.float32)
    s = jnp.where(valid, s, -1e30)
    p = jax.nn.softmax(s, axis=-1).astype(v.dtype)
    return jnp.einsum("bhqk,bkhd->bqhd", p, v)


def prompt_band_attention(q, k, v, rel_table):
    B, S = q.shape[0], q.shape[1]
    n_chunks = S // CHUNK
    band = LEFT_CTX + CHUNK
    pad = jnp.zeros((B, LEFT_CTX, ATT_HEADS, HEAD_DIM), k.dtype)
    k_pad = jnp.concatenate([pad, k], axis=1)
    v_pad = jnp.concatenate([pad.astype(v.dtype), v], axis=1)
    bias = rel_bias(rel_table, CHUNK, band, LEFT_CTX)

    def one_chunk(n):
        start = n * CHUNK
        q_c = lax.dynamic_slice_in_dim(q, start, CHUNK, axis=1)
        k_b = lax.dynamic_slice_in_dim(k_pad, start, band, axis=1)
        v_b = lax.dynamic_slice_in_dim(v_pad, start, band, axis=1)
        valid = jnp.arange(band) >= LEFT_CTX - start
        return band_softmax_attend(q_c, k_b, v_b, bias, valid)

    out = lax.map(one_chunk, jnp.arange(n_chunks))
    return jnp.moveaxis(out, 0, 1).reshape(B, S, ATT_DIM)


def sample_band_attention(q, k_new, v_new, cache_k, cache_v, rel_table):
    B, T = q.shape[0], q.shape[1]
    k_all = jnp.concatenate([cache_k.astype(k_new.dtype), k_new], axis=1)
    v_all = jnp.concatenate([cache_v.astype(v_new.dtype), v_new], axis=1)
    n_past = cache_k.shape[1]
    bias = rel_bias(rel_table, T, k_all.shape[1], n_past)
    valid = jnp.ones((k_all.shape[1],), dtype=bool)
    return band_softmax_attend(q, k_all, v_all, bias, valid).reshape(B, T, ATT_DIM)


def causal_short_conv(u, buf, conv_w):
    T = u.shape[1]
    full = jnp.concatenate([buf.astype(u.dtype), u], axis=1)
    y = full[:, 0:T] * conv_w[0]
    for j in range(1, CONV_WIDTH):
        y = y + full[:, j:j + T] * conv_w[j]
    return y, full[:, -(CONV_WIDTH - 1):]


def peer_channel_mixer(h, w_pq, sub_keys1, sub_keys2, expert_u, expert_v):
    lead = h.shape[:-1]
    hf = h.reshape(-1, D_MODEL)
    n = hf.shape[0]
    n_pad = (-n) % PEER_BLOCK
    blocks = jnp.pad(hf, ((0, n_pad), (0, 0))).reshape(-1, PEER_BLOCK, D_MODEL)

    def one_block(hb):
        q = (hb @ w_pq).reshape(PEER_BLOCK, PEER_HEADS, 2, PEER_HALF)
        s1 = jnp.einsum("thd,hkd->thk", q[:, :, 0], sub_keys1).astype(jnp.float32)
        s2 = jnp.einsum("thd,hkd->thk", q[:, :, 1], sub_keys2).astype(jnp.float32)
        v1, i1 = lax.top_k(s1, PEER_TOPK)
        v2, i2 = lax.top_k(s2, PEER_TOPK)
        cand = (v1[..., :, None] + v2[..., None, :]).reshape(PEER_BLOCK, PEER_HEADS, PEER_TOPK * PEER_TOPK)
        cand_idx = (i1[..., :, None] * PEER_NKEYS + i2[..., None, :]).reshape(PEER_BLOCK, PEER_HEADS, PEER_TOPK * PEER_TOPK)
        top_s, pos = lax.top_k(cand, PEER_TOPK)
        idx = jnp.take_along_axis(cand_idx, pos, axis=-1)
        g = jax.nn.softmax(top_s, axis=-1).astype(hb.dtype)
        u = jnp.take(expert_u, idx, axis=0)
        act = jax.nn.gelu(jnp.einsum("thkd,td->thk", u, hb))
        vv = jnp.take(expert_v, idx, axis=0)
        return jnp.einsum("thk,thkd->td", g * act, vv)

    out = lax.map(one_block, blocks).reshape(-1, D_MODEL)[:n]
    return out.reshape(*lead, D_MODEL)


def block_forward(x, c, conv_buf, attend, w_ada, b_ada, norm1_gain, norm2_gain, w_in, q_gain, k_gain,
                  conv_w, w_o_att, w_o_conv, w_o, w_pq, sub_keys1, sub_keys2, expert_u, expert_v):
    B, T = x.shape[0], x.shape[1]
    shift1, scale1, gate1, shift2, scale2, gate2 = adaln(c, w_ada, b_ada)
    h = rms_norm(x, norm1_gain) * (1 + scale1) + shift1
    z = h @ w_in
    q, k, v, cb, cc, cx, ga, gb = jnp.split(z, IN_SPLITS, axis=-1)
    q = rms_norm(q.reshape(B, T, ATT_HEADS, HEAD_DIM), q_gain)
    k = rms_norm(k.reshape(B, T, ATT_HEADS, HEAD_DIM), k_gain)
    v = v.reshape(B, T, ATT_HEADS, HEAD_DIM)
    y_att = attend(q, k, v) @ w_o_att
    conv_y, conv_state = causal_short_conv(cc * cx, conv_buf, conv_w)
    y_conv = (cb * conv_y) @ w_o_conv
    mixed = (jax.nn.sigmoid(ga) * y_att + jax.nn.sigmoid(gb) * y_conv) @ w_o
    x = x + gate1 * mixed
    h2 = rms_norm(x, norm2_gain) * (1 + scale2) + shift2
    x = x + gate2 * peer_channel_mixer(h2, w_pq, sub_keys1, sub_keys2, expert_u, expert_v)
    return x, k, v, conv_state


def setup_inputs(seed: int = 0) -> dict:
    key = jax.random.key(seed)
    ks = jax.random.split(key, 28)
    nrm = lambda k, shape, s: jax.random.normal(k, shape, jnp.float32) * s
    att_cache = min(LEFT_CTX, PAST_LEN)
    L = DEPTH
    return {
        "x_prompt": nrm(ks[0], (BATCH, SEQ, D_MODEL), 1.0),
        "c_prompt": nrm(ks[1], (BATCH, D_MODEL), 1.0),
        "x_sample": nrm(ks[2], (DEC_BATCH, DEC_SEQ, D_MODEL), 1.0),
        "c_sample": nrm(ks[3], (DEC_BATCH, D_MODEL), 1.0),
        "cache_k": nrm(ks[4], (L, DEC_BATCH, att_cache, ATT_HEADS, HEAD_DIM), 1.0),
        "cache_v": nrm(ks[5], (L, DEC_BATCH, att_cache, ATT_HEADS, HEAD_DIM), 1.0),
        "cache_conv": nrm(ks[6], (L, DEC_BATCH, CONV_WIDTH - 1, CONV_DIM), 0.5),
        "w_ada": nrm(ks[7], (L, D_MODEL, 6 * D_MODEL), D_MODEL ** -0.5),
        "b_ada": nrm(ks[8], (L, 6 * D_MODEL), 0.02),
        "norm1_gain": 1.0 + nrm(ks[9], (L, D_MODEL), 0.02),
        "norm2_gain": 1.0 + nrm(ks[10], (L, D_MODEL), 0.02),
        "w_in": nrm(ks[11], (L, D_MODEL, IN_COLS), D_MODEL ** -0.5),
        "q_gain": 1.0 + nrm(ks[12], (L, HEAD_DIM), 0.02),
        "k_gain": 1.0 + nrm(ks[13], (L, HEAD_DIM), 0.02),
        "rel_table": nrm(ks[14], (L, ATT_HEADS, N_REL), 0.2),
        "conv_w": nrm(ks[15], (L, CONV_WIDTH, CONV_DIM), CONV_WIDTH ** -0.5),
        "w_o_att": nrm(ks[16], (L, ATT_DIM, D_MODEL), ATT_DIM ** -0.5),
        "w_o_conv": nrm(ks[17], (L, CONV_DIM, D_MODEL), CONV_DIM ** -0.5),
        "w_o": nrm(ks[18], (L, D_MODEL, D_MODEL), D_MODEL ** -0.5),
        "w_pq": nrm(ks[19], (L, D_MODEL, PEER_HEADS * PEER_DKEY), D_MODEL ** -0.5),
        "sub_keys1": nrm(ks[20], (L, PEER_HEADS, PEER_NKEYS, PEER_HALF), PEER_HALF ** -0.5),
        "sub_keys2": nrm(ks[21], (L, PEER_HEADS, PEER_NKEYS, PEER_HALF), PEER_HALF ** -0.5),
        "expert_u": nrm(ks[22], (L, PEER_N, D_MODEL), D_MODEL ** -0.5),
        "expert_v": nrm(ks[23], (L, PEER_N, D_MODEL), 0.5),
    }


def reference(x_prompt, c_prompt, x_sample, c_sample, cache_k, cache_v, cache_conv,
              w_ada, b_ada, norm1_gain, norm2_gain, w_in, q_gain, k_gain, rel_table, conv_w,
              w_o_att, w_o_conv, w_o, w_pq, sub_keys1, sub_keys2, expert_u, expert_v):
    n_keep = min(LEFT_CTX, x_prompt.shape[1])
    xp, xs = x_prompt, x_sample
    kp_l, vp_l, cp_l, ks_l, vs_l, cs_l = [], [], [], [], [], []
    for l in range(DEPTH):
        layer_w = [w[l] for w in (w_ada, b_ada, norm1_gain, norm2_gain, w_in, q_gain, k_gain, conv_w,
                                  w_o_att, w_o_conv, w_o, w_pq, sub_keys1, sub_keys2, expert_u, expert_v)]
        rt = rel_table[l]
        conv_zero = jnp.zeros((xp.shape[0], CONV_WIDTH - 1, CONV_DIM), xp.dtype)
        xp, kp, vp, cp = block_forward(
            xp, c_prompt, conv_zero,
            lambda q, k, v, rt=rt: prompt_band_attention(q, k, v, rt), *layer_w)
        ck, cv = cache_k[l], cache_v[l]
        xs, kn, vn, cn = block_forward(
            xs, c_sample, cache_conv[l],
            lambda q, k, v, rt=rt, ck=ck, cv=cv: sample_band_attention(q, k, v, ck, cv, rt), *layer_w)
        kp_l.append(kp[:, -n_keep:])
        vp_l.append(vp[:, -n_keep:])
        cp_l.append(cp)
        ks_l.append(kn)
        vs_l.append(vn)
        cs_l.append(cn)
    return (xp, xs, jnp.stack(kp_l), jnp.stack(vp_l), jnp.stack(cp_l),
            jnp.stack(ks_l), jnp.stack(vs_l), jnp.stack(cs_l))
```

```python
import functools

import jax
import jax.numpy as jnp
from jax import lax
from jax.experimental import pallas as pl
from jax.experimental.pallas import tpu as pltpu

D_MODEL = 1024
CHUNK = 64
LEFT_CTX = 512
ATT_HEADS = 8
HEAD_DIM = 64
ATT_DIM = ATT_HEADS * HEAD_DIM
REL_CLIP = 128
CONV_DIM = 512
CONV_WIDTH = 3
PEER_HEADS = 8
PEER_NKEYS = 128
PEER_N = PEER_NKEYS * PEER_NKEYS
PEER_HALF = 128
PEER_TOPK = 16
EPS = 1e-6
IN_COLS = 3 * ATT_DIM + 3 * CONV_DIM + 2 * D_MODEL

LANES = 128
MIXER_TILE = 256
PEER_TILE = 512
PEER_EXPERT_BLOCK = 1024
VMEM_LIMIT = 56 * 1024 * 1024
NEG_BIG = -3.0e38
MASKED = -1e30

F32 = jnp.float32
BF16 = jnp.bfloat16
_NT = (((1,), (1,)), ((), ()))


def _const_spec(shape):
    zeros = (0,) * len(shape)
    return pl.BlockSpec(shape, lambda *_: zeros, pipeline_mode=pl.Buffered(1))


def _adaln_kernel(c_ref, w_ref, b_ref, o_ref):
    c = c_ref[...]
    s = c * jax.nn.sigmoid(c)
    o_ref[...] = jnp.dot(s.astype(BF16), w_ref[...].astype(BF16),
                         preferred_element_type=F32) + b_ref[...]


def _adaln(c_all, w_ada, b_ada):
    n = c_all.shape[0]
    cols = w_ada.shape[1]
    blk = 1024
    return pl.pallas_call(
        _adaln_kernel,
        grid=(cols // blk,),
        in_specs=[pl.BlockSpec((n, D_MODEL), lambda i: (0, 0)),
                  pl.BlockSpec((D_MODEL, blk), lambda i: (0, i)),
                  pl.BlockSpec((1, blk), lambda i: (0, i))],
        out_specs=pl.BlockSpec((n, blk), lambda i: (0, i)),
        out_shape=jax.ShapeDtypeStruct((n, cols), F32),
        compiler_params=pltpu.CompilerParams(dimension_semantics=("parallel",)),
        name="adaln",
    )(c_all, w_ada, b_ada.reshape(1, cols))


def _rms(x, gain_row):
    ms = jnp.mean(x * x, axis=-1, keepdims=True)
    return x * lax.rsqrt(ms + EPS) * gain_row


def _head_rms(t, seg, gain_row):
    sq = t * t
    hi = sq.astype(BF16)
    lo = (sq - hi.astype(F32)).astype(BF16)
    ms = (jnp.dot(hi, seg, preferred_element_type=F32)
          + jnp.dot(lo, seg, preferred_element_type=F32))
    return t * lax.rsqrt(ms + EPS) * gain_row


def _mixer_kernel(x_ref, mod_ref, ctxk_ref, ctxv_ref, cbuf_ref, g1_ref, win_ref, qg_ref, kg_ref,
                  seg_ref, bias_ref, cw_ref, woa_ref, woc_ref, wo_ref, g2_ref,
                  xmid_ref, h2_ref, kout_ref, vout_ref, cstate_ref,
                  kbuf, vbuf, ubuf, att_scr, *, tm, mask_pad):
    j = pl.program_id(1)
    band = LEFT_CTX + tm

    @pl.when(j == 0)
    def _():
        kbuf[0:LEFT_CTX, :] = ctxk_ref[0]
        vbuf[0:LEFT_CTX, :] = ctxv_ref[0]
        ubuf[0:8, :] = jnp.zeros((8, CONV_DIM), F32)
        ubuf[6:8, :] = cbuf_ref[0]

    x = x_ref[0]
    mod = mod_ref[0]
    shift1, scale1, gate1 = mod[0:1], mod[1:2], mod[2:3]
    shift2, scale2 = mod[3:4], mod[4:5]

    h = _rms(x, g1_ref[...]) * (1.0 + scale1) + shift1
    hb = h.astype(BF16)

    zq = jnp.dot(hb, win_ref[:, 0:3 * ATT_DIM], preferred_element_type=F32)
    seg = seg_ref[...]
    qn = _head_rms(zq[:, 0:ATT_DIM], seg, qg_ref[...])
    kn = _head_rms(zq[:, ATT_DIM:2 * ATT_DIM], seg, kg_ref[...])
    vv = zq[:, 2 * ATT_DIM:3 * ATT_DIM]
    kout_ref[0] = kn
    vout_ref[0] = vv
    kbuf[LEFT_CTX:band, :] = kn.astype(BF16)
    vbuf[LEFT_CTX:band, :] = vv.astype(BF16)
    qs = (qn * (HEAD_DIM ** -0.5)).astype(BF16)

    if mask_pad:
        col = lax.broadcasted_iota(jnp.int32, (tm, band), 1)
        valid = col >= (LEFT_CTX - j * tm)
    for hd in range(ATT_HEADS):
        hs = slice(hd * HEAD_DIM, (hd + 1) * HEAD_DIM)
        s = lax.dot_general(qs[:, hs], kbuf[:, hs], _NT, preferred_element_type=F32)
        s = s + bias_ref[hd]
        if mask_pad:
            s = jnp.where(valid, s, MASKED)
        m = jnp.max(s, axis=-1, keepdims=True)
        p = jnp.exp(s - m)
        l = jnp.sum(p, axis=-1, keepdims=True)
        o = jnp.dot(p.astype(BF16), vbuf[:, hs], preferred_element_type=F32)
        att_scr[:, hs] = o / l
    y_att = jnp.dot(att_scr[...].astype(BF16), woa_ref[...], preferred_element_type=F32)

    kbuf[0:LEFT_CTX, :] = kbuf[tm:tm + LEFT_CTX, :]
    vbuf[0:LEFT_CTX, :] = vbuf[tm:tm + LEFT_CTX, :]

    zc = jnp.dot(hb, win_ref[:, 3 * ATT_DIM:3 * ATT_DIM + 3 * CONV_DIM], preferred_element_type=F32)
    cb = zc[:, 0:CONV_DIM]
    ubuf[8:8 + tm, :] = zc[:, CONV_DIM:2 * CONV_DIM] * zc[:, 2 * CONV_DIM:3 * CONV_DIM]
    cw = cw_ref[...]
    y = ubuf[6:6 + tm, :] * cw[0:1]
    y = y + ubuf[7:7 + tm, :] * cw[1:2]
    y = y + ubuf[8:8 + tm, :] * cw[2:3]
    tail = ubuf[6 + tm:8 + tm, :]
    cstate_ref[0] = tail
    ubuf[6:8, :] = tail
    y_conv = jnp.dot((cb * y).astype(BF16), woc_ref[...], preferred_element_type=F32)

    zg = jnp.dot(hb, win_ref[:, 3 * ATT_DIM + 3 * CONV_DIM:IN_COLS], preferred_element_type=F32)
    merged = (jax.nn.sigmoid(zg[:, 0:D_MODEL]) * y_att
              + jax.nn.sigmoid(zg[:, D_MODEL:2 * D_MODEL]) * y_conv)
    mixed = jnp.dot(merged.astype(BF16), wo_ref[...], preferred_element_type=F32)
    x1 = x + gate1 * mixed
    xmid_ref[0] = x1
    h2 = _rms(x1, g2_ref[...]) * (1.0 + scale2) + shift2
    h2_ref[0] = h2.astype(BF16)


def _rel_bias(rel_table, tm):
    band = LEFT_CTX + tm
    qi = jnp.arange(tm)[:, None]
    kj = jnp.arange(band)[None, :]
    idx = jnp.clip(qi - kj + LEFT_CTX, -REL_CLIP, REL_CLIP) + REL_CLIP
    bias = rel_table[:, idx]
    lo = (qi // CHUNK) * CHUNK
    allowed = (kj >= lo) & (kj < lo + LEFT_CTX + CHUNK)
    return jnp.where(allowed[None], bias, MASKED).astype(F32)


def _mixer(x, mod, ctx_k, ctx_v, conv_buf, w, *, tm, mask_pad):
    nb, seq, _ = x.shape
    n_steps = seq // tm
    n_keep = min(LEFT_CTX, seq)
    first_kept = n_steps - n_keep // tm
    band = LEFT_CTX + tm
    bias = _rel_bias(w["rel_table"], tm)

    tile = lambda b, j: (b, j, 0)
    per_row = lambda b, j: (b, 0, 0)
    kept = lambda b, j: (b, jnp.maximum(j - first_kept, 0), 0)
    in_specs = [
        pl.BlockSpec((1, tm, D_MODEL), tile),
        pl.BlockSpec((1, 6, D_MODEL), per_row),
        pl.BlockSpec((1, LEFT_CTX, ATT_DIM), per_row),
        pl.BlockSpec((1, LEFT_CTX, ATT_DIM), per_row),
        pl.BlockSpec((1, CONV_WIDTH - 1, CONV_DIM), per_row),
        _const_spec((1, D_MODEL)),
        _const_spec((D_MODEL, IN_COLS)),
        _const_spec((1, ATT_DIM)),
        _const_spec((1, ATT_DIM)),
        _const_spec((ATT_DIM, ATT_DIM)),
        _const_spec((ATT_HEADS, tm, band)),
        _const_spec((CONV_WIDTH, CONV_DIM)),
        _const_spec((ATT_DIM, D_MODEL)),
        _const_spec((CONV_DIM, D_MODEL)),
        _const_spec((D_MODEL, D_MODEL)),
        _const_spec((1, D_MODEL)),
    ]
    out_specs = [
        pl.BlockSpec((1, tm, D_MODEL), tile),
        pl.BlockSpec((1, tm, D_MODEL), tile),
        pl.BlockSpec((1, tm, ATT_DIM), kept),
        pl.BlockSpec((1, tm, ATT_DIM), kept),
        pl.BlockSpec((1, CONV_WIDTH - 1, CONV_DIM), per_row),
    ]
    out_shape = [
        jax.ShapeDtypeStruct((nb, seq, D_MODEL), F32),
        jax.ShapeDtypeStruct((nb, seq, D_MODEL), BF16),
        jax.ShapeDtypeStruct((nb, n_keep, ATT_DIM), F32),
        jax.ShapeDtypeStruct((nb, n_keep, ATT_DIM), F32),
        jax.ShapeDtypeStruct((nb, CONV_WIDTH - 1, CONV_DIM), F32),
    ]
    scratch = [
        pltpu.VMEM((band, ATT_DIM), BF16),
        pltpu.VMEM((band, ATT_DIM), BF16),
        pltpu.VMEM((8 + tm, CONV_DIM), F32),
        pltpu.VMEM((tm, ATT_DIM), F32),
    ]
    return pl.pallas_call(
        functools.partial(_mixer_kernel, tm=tm, mask_pad=mask_pad),
        grid=(nb, n_steps),
        in_specs=in_specs,
        out_specs=out_specs,
        out_shape=out_shape,
        scratch_shapes=scratch,
        compiler_params=pltpu.CompilerParams(
            dimension_semantics=("parallel", "arbitrary"), vmem_limit_bytes=VMEM_LIMIT),
        name="mixer",
    )(x, mod, ctx_k, ctx_v, conv_buf, w["g1"], w["w_in"], w["q_gain"], w["k_gain"], w["seg"],
      bias, w["conv_w"], w["w_o_att"], w["w_o_conv"], w["w_o"], w["g2"])


def _gelu_tanh(x):
    c = 0.7978845608028654
    return x * (0.5 * (1.0 + jnp.tanh(c * (x + 0.044715 * (x * x * x)))))


def _top_values(cur, dst_ref, lanes, n):
    for r in range(n):
        m = jnp.max(cur, axis=0, keepdims=True)
        dst_ref[r:r + 1, lanes] = m
        cur = jnp.where(cur == m, NEG_BIG, cur)


_CAND_COUNTS = tuple(PEER_TOPK // (r1 + 1) for r1 in range(PEER_TOPK))
_N_CAND = sum(_CAND_COUNTS)
_CAND_ROWS = -(-_N_CAND // 8) * 8


def _peer_kernel(xmid_ref, h2_ref, mod_ref, wpq_ref, k1_ref, k2_ref, u_ref, vt_ref, out_ref,
                 h2s, at_scr, bt_scr, c1_scr, e2_scr, tau_scr, v1_scr, v2_scr, cand_scr, p_scr,
                 acc_scr, *, nb, ts):
    j = pl.program_id(1)
    tm = nb * ts
    n_lane_tiles = tm // LANES
    sub_blocks = PEER_EXPERT_BLOCK // PEER_NKEYS

    @pl.when(j == 0)
    def _retrieve():
        for b in range(nb):
            h2s[b * ts:(b + 1) * ts, :] = h2_ref[b]
        acc_scr[...] = jnp.zeros(acc_scr.shape, F32)
        h2v = h2s[...]

        def per_head(hd, carry):
            col = pl.multiple_of(hd * (2 * PEER_HALF), 2 * PEER_HALF)
            q = jnp.dot(h2v, wpq_ref[:, pl.ds(col, 2 * PEER_HALF)], preferred_element_type=F32)
            q1 = q[:, 0:PEER_HALF].astype(BF16)
            q2 = q[:, PEER_HALF:2 * PEER_HALF].astype(BF16)
            at_scr[hd] = lax.dot_general(k1_ref[hd], q1, _NT, preferred_element_type=F32)
            bt_scr[hd] = lax.dot_general(k2_ref[hd], q2, _NT, preferred_element_type=F32)
            for lt in range(n_lane_tiles):
                lanes = slice(lt * LANES, (lt + 1) * LANES)
                s1 = at_scr[hd, :, lanes]
                s2 = bt_scr[hd, :, lanes]
                _top_values(s1, v1_scr, lanes, PEER_TOPK)
                _top_values(s2, v2_scr, lanes, PEER_TOPK)
                v1 = v1_scr[:, lanes]
                v2 = v2_scr[:, lanes]
                off = 0
                for r1, cnt in enumerate(_CAND_COUNTS):
                    cand_scr[off:off + cnt, lanes] = v1[r1:r1 + 1] + v2[0:cnt]
                    off += cnt
                if _CAND_ROWS > _N_CAND:
                    cand_scr[_N_CAND:_CAND_ROWS, lanes] = jnp.full(
                        (_CAND_ROWS - _N_CAND, LANES), NEG_BIG, F32)
                cur = cand_scr[:, lanes]
                top = v1[0:1] + v2[0:1]
                z = jnp.zeros((1, LANES), F32)
                m = top
                for _ in range(PEER_TOPK):
                    m = jnp.max(cur, axis=0, keepdims=True)
                    z = z + jnp.exp(m - top)
                    cur = jnp.where(cur == m, NEG_BIG, cur)
                tau_scr[hd, 0:1, lanes] = m
                c1_scr[hd, :, lanes] = jnp.exp(s1 - v1[0:1]) / z
                e2_scr[hd, :, lanes] = jnp.exp(s2 - v2[0:1])
            return carry

        lax.fori_loop(0, PEER_HEADS, per_head, 0)

    h2v = h2s[...]
    row0 = pl.multiple_of(j * sub_blocks, sub_blocks)
    for i in range(sub_blocks):
        act = lax.dot_general(u_ref[i * PEER_NKEYS:(i + 1) * PEER_NKEYS, :], h2v, _NT,
                              preferred_element_type=F32)
        for lt in range(n_lane_tiles):
            lanes = slice(lt * LANES, (lt + 1) * LANES)

            def per_head_w(hd, w):
                a = at_scr[hd, pl.ds(row0, sub_blocks), lanes][i:i + 1]
                c = c1_scr[hd, pl.ds(row0, sub_blocks), lanes][i:i + 1]
                t = tau_scr[hd, 0:1, lanes]
                s = a + bt_scr[hd, :, lanes]
                return w + jnp.where(s >= t, e2_scr[hd, :, lanes], 0.0) * c

            w = lax.fori_loop(0, PEER_HEADS, per_head_w, jnp.zeros((PEER_NKEYS, LANES), F32))
            p_scr[i * PEER_NKEYS:(i + 1) * PEER_NKEYS, lanes] = (
                w * _gelu_tanh(act[:, lanes])).astype(BF16)
    acc_scr[...] += jnp.dot(vt_ref[...], p_scr[...], preferred_element_type=F32)

    @pl.when(j == pl.num_programs(1) - 1)
    def _finish():
        peer = acc_scr[...].T
        mod = mod_ref[...]
        for b in range(nb):
            gate2 = mod[b, 5:6, :]
            out_ref[b] = xmid_ref[b] + gate2 * peer[b * ts:(b + 1) * ts, :]


def _peer(x_mid, h2, mod, w, *, nb, ts):
    n_rows, seq, _ = x_mid.shape
    tm = nb * ts
    tiles_per_row = seq // ts
    n_tiles = (n_rows // nb) * tiles_per_row
    n_blocks = PEER_N // PEER_EXPERT_BLOCK

    tok = lambda i, j: (i // tiles_per_row, i % tiles_per_row, 0)
    row = lambda i, j: (i // tiles_per_row, 0, 0)
    in_specs = [
        pl.BlockSpec((nb, ts, D_MODEL), tok),
        pl.BlockSpec((nb, ts, D_MODEL), tok),
        pl.BlockSpec((nb, 6, D_MODEL), row),
        _const_spec((D_MODEL, PEER_HEADS * 2 * PEER_HALF)),
        _const_spec((PEER_HEADS, PEER_NKEYS, PEER_HALF)),
        _const_spec((PEER_HEADS, PEER_NKEYS, PEER_HALF)),
        pl.BlockSpec((PEER_EXPERT_BLOCK, D_MODEL), lambda i, j: (j, 0)),
        pl.BlockSpec((D_MODEL, PEER_EXPERT_BLOCK), lambda i, j: (0, j)),
    ]
    per_head = (PEER_HEADS, PEER_NKEYS, tm)
    scratch = [
        pltpu.VMEM((tm, D_MODEL), BF16),
        pltpu.VMEM(per_head, F32),
        pltpu.VMEM(per_head, F32),
        pltpu.VMEM(per_head, F32),
        pltpu.VMEM(per_head, F32),
        pltpu.VMEM((PEER_HEADS, 8, tm), F32),
        pltpu.VMEM((PEER_TOPK, tm), F32),
        pltpu.VMEM((PEER_TOPK, tm), F32),
        pltpu.VMEM((_CAND_ROWS, tm), F32),
        pltpu.VMEM((PEER_EXPERT_BLOCK, tm), BF16),
        pltpu.VMEM((D_MODEL, tm), F32),
    ]
    return pl.pallas_call(
        functools.partial(_peer_kernel, nb=nb, ts=ts),
        grid=(n_tiles, n_blocks),
        in_specs=in_specs,
        out_specs=pl.BlockSpec((nb, ts, D_MODEL), tok),
        out_shape=jax.ShapeDtypeStruct(x_mid.shape, F32),
        scratch_shapes=scratch,
        compiler_params=pltpu.CompilerParams(
            dimension_semantics=("parallel", "arbitrary"), vmem_limit_bytes=VMEM_LIMIT),
        name="peer",
    )(x_mid, h2, mod, w["w_pq"], w["sub_keys1"], w["sub_keys2"], w["expert_u"], w["expert_vt"])


def _layer_weights(l, norm1_gain, norm2_gain, w_in, q_gain, k_gain, rel_table, conv_w, w_o_att,
                   w_o_conv, w_o, w_pq, sub_keys1, sub_keys2, expert_u, expert_v):
    head_of = jnp.arange(ATT_DIM) // HEAD_DIM
    seg = jnp.where(head_of[:, None] == head_of[None, :], 1.0 / HEAD_DIM, 0.0).astype(BF16)
    return {
        "g1": norm1_gain[l].reshape(1, D_MODEL),
        "g2": norm2_gain[l].reshape(1, D_MODEL),
        "w_in": w_in[l].astype(BF16),
        "q_gain": jnp.tile(q_gain[l], ATT_HEADS).reshape(1, ATT_DIM),
        "k_gain": jnp.tile(k_gain[l], ATT_HEADS).reshape(1, ATT_DIM),
        "seg": seg,
        "rel_table": rel_table[l],
        "conv_w": conv_w[l],
        "w_o_att": w_o_att[l].astype(BF16),
        "w_o_conv": w_o_conv[l].astype(BF16),
        "w_o": w_o[l].astype(BF16),
        "w_pq": w_pq[l].astype(BF16),
        "sub_keys1": sub_keys1[l].astype(BF16),
        "sub_keys2": sub_keys2[l].astype(BF16),
        "expert_u": expert_u[l].astype(BF16),
        "expert_vt": expert_v[l].astype(BF16).T,
    }


def kernel(x_prompt, c_prompt, x_sample, c_sample, cache_k, cache_v, cache_conv, w_ada, b_ada, norm1_gain, norm2_gain, w_in, q_gain, k_gain, rel_table, conv_w, w_o_att, w_o_conv, w_o, w_pq, sub_keys1, sub_keys2, expert_u, expert_v):
    depth = w_ada.shape[0]
    nbp, seq_p, _ = x_prompt.shape
    nbs, seq_s, _ = x_sample.shape
    xp, xs = x_prompt, x_sample
    c_all = jnp.concatenate([c_prompt, c_sample], axis=0)
    outs = [[] for _ in range(6)]
    for l in range(depth):
        w = _layer_weights(l, norm1_gain, norm2_gain, w_in, q_gain, k_gain, rel_table, conv_w,
                           w_o_att, w_o_conv, w_o, w_pq, sub_keys1, sub_keys2, expert_u, expert_v)
        mod = _adaln(c_all, w_ada[l], b_ada[l]).reshape(nbp + nbs, 6, D_MODEL)
        mod_p, mod_s = mod[:nbp], mod[nbp:]

        zeros_ctx = jnp.zeros((nbp, LEFT_CTX, ATT_DIM), BF16)
        zeros_conv = jnp.zeros((nbp, CONV_WIDTH - 1, CONV_DIM), F32)
        xp_mid, h2p, kp, vp, cp = _mixer(xp, mod_p, zeros_ctx, zeros_ctx, zeros_conv, w,
                                         tm=MIXER_TILE, mask_pad=True)
        ck = cache_k[l].reshape(nbs, -1, ATT_DIM).astype(BF16)
        cv = cache_v[l].reshape(nbs, -1, ATT_DIM).astype(BF16)
        xs_mid, h2s, kn, vn, cn = _mixer(xs, mod_s, ck, cv, cache_conv[l], w,
                                         tm=seq_s, mask_pad=False)

        xp = _peer(xp_mid, h2p, mod_p, w, nb=1, ts=PEER_TILE)
        xs = _peer(xs_mid, h2s, mod_s, w, nb=PEER_TILE // seq_s, ts=seq_s)

        for lst, val in zip(outs, (kp, vp, cp, kn, vn, cn)):
            lst.append(val)
    kp, vp, cp, kn, vn, cn = (jnp.stack(v) for v in outs)
    heads = (ATT_HEADS, HEAD_DIM)
    return (xp, xs,
            kp.reshape(kp.shape[:3] + heads), vp.reshape(vp.shape[:3] + heads), cp,
            kn.reshape(kn.shape[:3] + heads), vn.reshape(vn.shape[:3] + heads), cn)
```

```python
import functools

import jax
import jax.numpy as jnp
import numpy as np
from jax import lax
from jax.experimental import pallas as pl
from jax.experimental.pallas import tpu as pltpu

D_MODEL = 1024
CHUNK = 64
LEFT_CTX = 512
ATT_HEADS = 8
HEAD_DIM = 64
ATT_DIM = ATT_HEADS * HEAD_DIM
REL_CLIP = 128
CONV_DIM = 512
CONV_WIDTH = 3
PEER_HEADS = 8
PEER_NKEYS = 128
PEER_N = PEER_NKEYS * PEER_NKEYS
PEER_HALF = 128
PEER_TOPK = 16
EPS = 1e-6
IN_COLS = 3 * ATT_DIM + 3 * CONV_DIM + 2 * D_MODEL

LANES = 128
MIXER_TILE = 256
PEER_TILE = 512
PEER_EXPERT_BLOCK = 1024
VMEM_LIMIT = 56 * 1024 * 1024
NEG_BIG = -3.0e38
MASKED = -1e30

F32 = jnp.float32
BF16 = jnp.bfloat16
_NT = (((1,), (1,)), ((), ()))


def _const_spec(shape):
    zeros = (0,) * len(shape)
    return pl.BlockSpec(shape, lambda *_: zeros, pipeline_mode=pl.Buffered(1))


def _adaln_kernel(c_ref, w_ref, b_ref, o_ref):
    c = c_ref[...]
    s = c * jax.nn.sigmoid(c)
    o_ref[...] = jnp.dot(s.astype(BF16), w_ref[...].astype(BF16),
                         preferred_element_type=F32) + b_ref[...]


def _adaln(c_all, w_ada, b_ada):
    n = c_all.shape[0]
    cols = w_ada.shape[1]
    blk = 1024
    return pl.pallas_call(
        _adaln_kernel,
        grid=(cols // blk,),
        in_specs=[pl.BlockSpec((n, D_MODEL), lambda i: (0, 0)),
                  pl.BlockSpec((D_MODEL, blk), lambda i: (0, i)),
                  pl.BlockSpec((1, blk), lambda i: (0, i))],
        out_specs=pl.BlockSpec((n, blk), lambda i: (0, i)),
        out_shape=jax.ShapeDtypeStruct((n, cols), F32),
        compiler_params=pltpu.CompilerParams(dimension_semantics=("parallel",)),
        name="adaln",
    )(c_all, w_ada, b_ada.reshape(1, cols))


def _rms(x, gain_row):
    ms = jnp.mean(x * x, axis=-1, keepdims=True)
    return x * lax.rsqrt(ms + EPS) * gain_row


def _head_rms(t, seg, gain_row):
    sq = t * t
    hi = sq.astype(BF16)
    lo = (sq - hi.astype(F32)).astype(BF16)
    ms = (jnp.dot(hi, seg, preferred_element_type=F32)
          + jnp.dot(lo, seg, preferred_element_type=F32))
    return t * lax.rsqrt(ms + EPS) * gain_row


def _mixer_kernel(x_ref, mod_ref, ctxk_ref, ctxv_ref, cbuf_ref, g1_ref, win_ref, qg_ref, kg_ref,
                  seg_ref, bias_ref, cw_ref, woa_ref, woc_ref, wo_ref, g2_ref,
                  xmid_ref, h2_ref, kout_ref, vout_ref, cstate_ref,
                  kbuf, vbuf, ubuf, att_scr, *, tm, mask_pad):
    j = pl.program_id(1)
    band = LEFT_CTX + tm

    @pl.when(j == 0)
    def _():
        kbuf[0:LEFT_CTX, :] = ctxk_ref[0]
        vbuf[0:LEFT_CTX, :] = ctxv_ref[0]
        ubuf[0:8, :] = jnp.zeros((8, CONV_DIM), F32)
        ubuf[6:8, :] = cbuf_ref[0]

    x = x_ref[0]
    mod = mod_ref[0]
    shift1, scale1, gate1 = mod[0:1], mod[1:2], mod[2:3]
    shift2, scale2 = mod[3:4], mod[4:5]

    h = _rms(x, g1_ref[...]) * (1.0 + scale1) + shift1
    hb = h.astype(BF16)

    zq = jnp.dot(hb, win_ref[:, 0:3 * ATT_DIM], preferred_element_type=F32)
    seg = seg_ref[...]
    qn = _head_rms(zq[:, 0:ATT_DIM], seg, qg_ref[...])
    kn = _head_rms(zq[:, ATT_DIM:2 * ATT_DIM], seg, kg_ref[...])
    vv = zq[:, 2 * ATT_DIM:3 * ATT_DIM]
    kout_ref[0] = kn
    vout_ref[0] = vv
    kbuf[LEFT_CTX:band, :] = kn.astype(BF16)
    vbuf[LEFT_CTX:band, :] = vv.astype(BF16)
    qs = (qn * (HEAD_DIM ** -0.5)).astype(BF16)

    if mask_pad:
        col = lax.broadcasted_iota(jnp.int32, (tm, band), 1)
        valid = col >= (LEFT_CTX - j * tm)
    for hd in range(ATT_HEADS):
        hs = slice(hd * HEAD_DIM, (hd + 1) * HEAD_DIM)
        s = lax.dot_general(qs[:, hs], kbuf[:, hs], _NT, preferred_element_type=F32)
        s = s + bias_ref[hd]
        if mask_pad:
            s = jnp.where(valid, s, MASKED)
        m = jnp.max(s, axis=-1, keepdims=True)
        p = jnp.exp(s - m)
        l = jnp.sum(p, axis=-1, keepdims=True)
        o = jnp.dot(p.astype(BF16), vbuf[:, hs], preferred_element_type=F32)
        att_scr[:, hs] = o / l
    y_att = jnp.dot(att_scr[...].astype(BF16), woa_ref[...], preferred_element_type=F32)

    kbuf[0:LEFT_CTX, :] = kbuf[tm:tm + LEFT_CTX, :]
    vbuf[0:LEFT_CTX, :] = vbuf[tm:tm + LEFT_CTX, :]

    zc = jnp.dot(hb, win_ref[:, 3 * ATT_DIM:3 * ATT_DIM + 3 * CONV_DIM], preferred_element_type=F32)
    cb = zc[:, 0:CONV_DIM]
    ubuf[8:8 + tm, :] = zc[:, CONV_DIM:2 * CONV_DIM] * zc[:, 2 * CONV_DIM:3 * CONV_DIM]
    cw = cw_ref[...]
    y = ubuf[6:6 + tm, :] * cw[0:1]
    y = y + ubuf[7:7 + tm, :] * cw[1:2]
    y = y + ubuf[8:8 + tm, :] * cw[2:3]
    tail = ubuf[6 + tm:8 + tm, :]
    cstate_ref[0] = tail
    ubuf[6:8, :] = tail
    y_conv = jnp.dot((cb * y).astype(BF16), woc_ref[...], preferred_element_type=F32)

    zg = jnp.dot(hb, win_ref[:, 3 * ATT_DIM + 3 * CONV_DIM:IN_COLS], preferred_element_type=F32)
    merged = (jax.nn.sigmoid(zg[:, 0:D_MODEL]) * y_att
              + jax.nn.sigmoid(zg[:, D_MODEL:2 * D_MODEL]) * y_conv)
    mixed = jnp.dot(merged.astype(BF16), wo_ref[...], preferred_element_type=F32)
    x1 = x + gate1 * mixed
    xmid_ref[0] = x1
    h2 = _rms(x1, g2_ref[...]) * (1.0 + scale2) + shift2
    h2_ref[0] = h2.astype(BF16)


def _rel_bias(rel_table, tm):
    band = LEFT_CTX + tm
    period = band + tm - 1
    n_far_past = LEFT_CTX - REL_CLIP + tm - 1
    n_future = max(period - n_far_past - (2 * REL_CLIP + 1), 0)
    nh = rel_table.shape[0]
    by_offset = jnp.concatenate([
        jnp.broadcast_to(rel_table[:, -1:], (nh, n_far_past)),
        rel_table[:, ::-1],
        jnp.broadcast_to(rel_table[:, :1], (nh, n_future))], axis=1)[:, :period]
    skew = jnp.tile(by_offset, (1, tm + 1))[:, :tm * (period + 1)].reshape(nh, tm, period + 1)
    bias = skew[:, ::-1, :band]
    qi = np.arange(tm)[:, None]
    kj = np.arange(band)[None, :]
    lo = (qi // CHUNK) * CHUNK
    allowed = (kj >= lo) & (kj < lo + LEFT_CTX + CHUNK)
    return jnp.where(allowed[None], bias, MASKED).astype(F32)


def _mixer(x, mod, ctx_k, ctx_v, conv_buf, w, *, tm, mask_pad):
    nb, seq, _ = x.shape
    n_steps = seq // tm
    n_keep = min(LEFT_CTX, seq)
    first_kept = n_steps - n_keep // tm
    band = LEFT_CTX + tm
    bias = _rel_bias(w["rel_table"], tm)

    tile = lambda b, j: (b, j, 0)
    per_row = lambda b, j: (b, 0, 0)
    kept = lambda b, j: (b, jnp.maximum(j - first_kept, 0), 0)
    in_specs = [
        pl.BlockSpec((1, tm, D_MODEL), tile),
        pl.BlockSpec((1, 6, D_MODEL), per_row),
        pl.BlockSpec((1, LEFT_CTX, ATT_DIM), per_row),
        pl.BlockSpec((1, LEFT_CTX, ATT_DIM), per_row),
        pl.BlockSpec((1, CONV_WIDTH - 1, CONV_DIM), per_row),
        _const_spec((1, D_MODEL)),
        _const_spec((D_MODEL, IN_COLS)),
        _const_spec((1, ATT_DIM)),
        _const_spec((1, ATT_DIM)),
        _const_spec((ATT_DIM, ATT_DIM)),
        _const_spec((ATT_HEADS, tm, band)),
        _const_spec((CONV_WIDTH, CONV_DIM)),
        _const_spec((ATT_DIM, D_MODEL)),
        _const_spec((CONV_DIM, D_MODEL)),
        _const_spec((D_MODEL, D_MODEL)),
        _const_spec((1, D_MODEL)),
    ]
    out_specs = [
        pl.BlockSpec((1, tm, D_MODEL), tile),
        pl.BlockSpec((1, tm, D_MODEL), tile),
        pl.BlockSpec((1, tm, ATT_DIM), kept),
        pl.BlockSpec((1, tm, ATT_DIM), kept),
        pl.BlockSpec((1, CONV_WIDTH - 1, CONV_DIM), per_row),
    ]
    out_shape = [
        jax.ShapeDtypeStruct((nb, seq, D_MODEL), F32),
        jax.ShapeDtypeStruct((nb, seq, D_MODEL), BF16),
        jax.ShapeDtypeStruct((nb, n_keep, ATT_DIM), F32),
        jax.ShapeDtypeStruct((nb, n_keep, ATT_DIM), F32),
        jax.ShapeDtypeStruct((nb, CONV_WIDTH - 1, CONV_DIM), F32),
    ]
    scratch = [
        pltpu.VMEM((band, ATT_DIM), BF16),
        pltpu.VMEM((band, ATT_DIM), BF16),
        pltpu.VMEM((8 + tm, CONV_DIM), F32),
        pltpu.VMEM((tm, ATT_DIM), F32),
    ]
    return pl.pallas_call(
        functools.partial(_mixer_kernel, tm=tm, mask_pad=mask_pad),
        grid=(nb, n_steps),
        in_specs=in_specs,
        out_specs=out_specs,
        out_shape=out_shape,
        scratch_shapes=scratch,
        compiler_params=pltpu.CompilerParams(
            dimension_semantics=("parallel", "arbitrary"), vmem_limit_bytes=VMEM_LIMIT),
        name="mixer",
    )(x, mod, ctx_k, ctx_v, conv_buf, w["g1"], w["w_in"], w["q_gain"], w["k_gain"], w["seg"],
      bias, w["conv_w"], w["w_o_att"], w["w_o_conv"], w["w_o"], w["g2"])


def _gelu_tanh(x):
    c = 0.7978845608028654
    return x * (0.5 * (1.0 + jnp.tanh(c * (x + 0.044715 * (x * x * x)))))


def _top_values(cur, dst_ref, lanes, n):
    rank = jnp.full(cur.shape, float(n), F32)
    for r in range(n):
        m = jnp.max(cur, axis=0, keepdims=True)
        dst_ref[r:r + 1, lanes] = m
        hit = cur == m
        rank = jnp.where(hit, float(r), rank)
        cur = jnp.where(hit, NEG_BIG, cur)
    return rank


_CAND_COUNTS = tuple(PEER_TOPK // (r1 + 1) for r1 in range(PEER_TOPK))
_N_CAND = sum(_CAND_COUNTS)
_CAND_ROWS = -(-_N_CAND // 8) * 8


def _pair_bf16(x):
    hi = pltpu.bitcast(x.astype(BF16).astype(F32), jnp.uint32)
    return pltpu.bitcast(hi | lax.shift_right_logical(hi, jnp.uint32(16)), F32)


def _bcast_rows_bf16(pair_row):
    return pltpu.bitcast(jnp.broadcast_to(pair_row, (PEER_NKEYS // 2, LANES)), BF16)


def _peer_kernel(xmid_ref, h2_ref, mod_ref, wpq_ref, k1_ref, k2_ref, u_ref, vt_ref, out_ref,
                 h2s, s1_scr, s2_scr, npk_scr, cpk_scr, rb_scr, e2_scr, v1_scr, v2_scr, cand_scr,
                 act_scr, p_scr, acc_scr, *, nb, ts):
    j = pl.program_id(1)
    tm = nb * ts
    n_lane_tiles = tm // LANES
    sub_blocks = PEER_EXPERT_BLOCK // PEER_NKEYS

    @pl.when(j == 0)
    def _retrieve():
        for b in range(nb):
            h2s[b * ts:(b + 1) * ts, :] = h2_ref[b]
        acc_scr[...] = jnp.zeros(acc_scr.shape, F32)
        h2v = h2s[...]

        def per_head(hd, carry):
            col = pl.multiple_of(hd * (2 * PEER_HALF), 2 * PEER_HALF)
            q = jnp.dot(h2v, wpq_ref[:, pl.ds(col, 2 * PEER_HALF)], preferred_element_type=F32)
            q1 = q[:, 0:PEER_HALF].astype(BF16)
            q2 = q[:, PEER_HALF:2 * PEER_HALF].astype(BF16)
            s1_scr[...] = lax.dot_general(k1_ref[hd], q1, _NT, preferred_element_type=F32)
            s2_scr[...] = lax.dot_general(k2_ref[hd], q2, _NT, preferred_element_type=F32)
            for lt in range(n_lane_tiles):
                lanes = slice(lt * LANES, (lt + 1) * LANES)
                s1 = s1_scr[:, lanes]
                s2 = s2_scr[:, lanes]
                _top_values(s1, v1_scr, lanes, PEER_TOPK)
                rank2 = _top_values(s2, v2_scr, lanes, PEER_TOPK)
                v1 = v1_scr[:, lanes]
                v2 = v2_scr[:, lanes]
                off = 0
                for r1, cnt in enumerate(_CAND_COUNTS):
                    cand_scr[off:off + cnt, lanes] = v1[r1:r1 + 1] + v2[0:cnt]
                    off += cnt
                if _CAND_ROWS > _N_CAND:
                    cand_scr[_N_CAND:_CAND_ROWS, lanes] = jnp.full(
                        (_CAND_ROWS - _N_CAND, LANES), NEG_BIG, F32)
                cur = cand_scr[:, lanes]
                top = v1[0:1] + v2[0:1]
                z = jnp.zeros((1, LANES), F32)
                tau = top
                for _ in range(PEER_TOPK):
                    tau = jnp.max(cur, axis=0, keepdims=True)
                    z = z + jnp.exp(tau - top)
                    cur = jnp.where(cur == tau, NEG_BIG, cur)
                count = jnp.zeros((PEER_NKEYS, LANES), F32)
                for r in range(PEER_TOPK):
                    v1r = v1[r:r + 1]
                    n_r = jnp.sum(jnp.where(v1r + v2 >= tau, 1.0, 0.0), axis=0, keepdims=True)
                    count = jnp.where(s1 == v1r, n_r, count)
                npk_scr[hd, :, lanes] = _pair_bf16(count)
                cpk_scr[hd, :, lanes] = _pair_bf16(jnp.exp(s1 - v1[0:1]) / z)
                rb_scr[hd, :, lanes] = rank2.astype(BF16)
                e2_scr[hd, :, lanes] = jnp.exp(s2 - v2[0:1]).astype(BF16)
            return carry

        lax.fori_loop(0, PEER_HEADS, per_head, 0)

    row0 = pl.multiple_of(j * sub_blocks, sub_blocks)
    group = 2
    chunk = group * PEER_NKEYS
    n_chunks = sub_blocks // group

    def pre_activations(c):
        rows = slice(c * chunk, (c + 1) * chunk)
        act_scr[rows, :] = lax.dot_general(u_ref[rows, :], h2s[...], _NT, preferred_element_type=F32)

    pre_activations(0)
    for c in range(n_chunks):
        if c + 1 < n_chunks:
            pre_activations(c + 1)
        for lt in range(n_lane_tiles):
            lanes = slice(lt * LANES, (lt + 1) * LANES)
            ws = [jnp.zeros((PEER_NKEYS, LANES), BF16) for _ in range(group)]
            for hd in range(PEER_HEADS):
                counts = npk_scr[hd, pl.ds(row0, sub_blocks), lanes]
                coefs = cpk_scr[hd, pl.ds(row0, sub_blocks), lanes]
                rank2 = rb_scr[hd, :, lanes]
                e2 = e2_scr[hd, :, lanes]
                for g in range(group):
                    i = c * group + g
                    count = _bcast_rows_bf16(counts[i:i + 1])
                    coef = _bcast_rows_bf16(coefs[i:i + 1])
                    ws[g] = ws[g] + jnp.where(rank2 < count, e2, jnp.zeros_like(e2)) * coef
            for g in range(group):
                rows = slice((c * group + g) * PEER_NKEYS, (c * group + g + 1) * PEER_NKEYS)
                act = _gelu_tanh(act_scr[rows, lanes])
                p_scr[rows, lanes] = (ws[g].astype(F32) * act).astype(BF16)
        rows = slice(c * chunk, (c + 1) * chunk)
        acc_scr[...] += jnp.dot(vt_ref[:, rows], p_scr[rows, :], preferred_element_type=F32)

    @pl.when(j == pl.num_programs(1) - 1)
    def _finish():
        peer = acc_scr[...].T
        mod = mod_ref[...]
        for b in range(nb):
            gate2 = mod[b, 5:6, :]
            out_ref[b] = xmid_ref[b] + gate2 * peer[b * ts:(b + 1) * ts, :]


def _peer(x_mid, h2, mod, w, *, nb, ts):
    n_rows, seq, _ = x_mid.shape
    tm = nb * ts
    tiles_per_row = seq // ts
    n_tiles = (n_rows // nb) * tiles_per_row
    n_blocks = PEER_N // PEER_EXPERT_BLOCK

    tok = lambda i, j: (i // tiles_per_row, i % tiles_per_row, 0)
    row = lambda i, j: (i // tiles_per_row, 0, 0)
    in_specs = [
        pl.BlockSpec((nb, ts, D_MODEL), tok),
        pl.BlockSpec((nb, ts, D_MODEL), tok),
        pl.BlockSpec((nb, 6, D_MODEL), row),
        _const_spec((D_MODEL, PEER_HEADS * 2 * PEER_HALF)),
        _const_spec((PEER_HEADS, PEER_NKEYS, PEER_HALF)),
        _const_spec((PEER_HEADS, PEER_NKEYS, PEER_HALF)),
        pl.BlockSpec((PEER_EXPERT_BLOCK, D_MODEL), lambda i, j: (j, 0)),
        pl.BlockSpec((D_MODEL, PEER_EXPERT_BLOCK), lambda i, j: (0, j)),
    ]
    per_head = (PEER_HEADS, PEER_NKEYS, tm)
    scratch = [
        pltpu.VMEM((tm, D_MODEL), BF16),
        pltpu.VMEM((PEER_NKEYS, tm), F32),
        pltpu.VMEM((PEER_NKEYS, tm), F32),
        pltpu.VMEM(per_head, F32),
        pltpu.VMEM(per_head, F32),
        pltpu.VMEM(per_head, BF16),
        pltpu.VMEM(per_head, BF16),
        pltpu.VMEM((PEER_TOPK, tm), F32),
        pltpu.VMEM((PEER_TOPK, tm), F32),
        pltpu.VMEM((_CAND_ROWS, tm), F32),
        pltpu.VMEM((PEER_EXPERT_BLOCK, tm), F32),
        pltpu.VMEM((PEER_EXPERT_BLOCK, tm), BF16),
        pltpu.VMEM((D_MODEL, tm), F32),
    ]
    return pl.pallas_call(
        functools.partial(_peer_kernel, nb=nb, ts=ts),
        grid=(n_tiles, n_blocks),
        in_specs=in_specs,
        out_specs=pl.BlockSpec((nb, ts, D_MODEL), tok),
        out_shape=jax.ShapeDtypeStruct(x_mid.shape, F32),
        scratch_shapes=scratch,
        compiler_params=pltpu.CompilerParams(
            dimension_semantics=("parallel", "arbitrary"), vmem_limit_bytes=VMEM_LIMIT),
        name="peer",
    )(x_mid, h2, mod, w["w_pq"], w["sub_keys1"], w["sub_keys2"], w["expert_u"], w["expert_vt"])


def _layer_weights(l, norm1_gain, norm2_gain, w_in, q_gain, k_gain, rel_table, conv_w, w_o_att,
                   w_o_conv, w_o, w_pq, sub_keys1, sub_keys2, expert_u, expert_v):
    head_of = jnp.arange(ATT_DIM) // HEAD_DIM
    seg = jnp.where(head_of[:, None] == head_of[None, :], 1.0 / HEAD_DIM, 0.0).astype(BF16)
    return {
        "g1": norm1_gain[l].reshape(1, D_MODEL),
        "g2": norm2_gain[l].reshape(1, D_MODEL),
        "w_in": w_in[l].astype(BF16),
        "q_gain": jnp.tile(q_gain[l], ATT_HEADS).reshape(1, ATT_DIM),
        "k_gain": jnp.tile(k_gain[l], ATT_HEADS).reshape(1, ATT_DIM),
        "seg": seg,
        "rel_table": rel_table[l],
        "conv_w": conv_w[l],
        "w_o_att": w_o_att[l].astype(BF16),
        "w_o_conv": w_o_conv[l].astype(BF16),
        "w_o": w_o[l].astype(BF16),
        "w_pq": w_pq[l].astype(BF16),
        "sub_keys1": sub_keys1[l].astype(BF16),
        "sub_keys2": sub_keys2[l].astype(BF16),
        "expert_u": expert_u[l].astype(BF16),
        "expert_vt": expert_v[l].astype(BF16).T,
    }


def kernel(x_prompt, c_prompt, x_sample, c_sample, cache_k, cache_v, cache_conv, w_ada, b_ada, norm1_gain, norm2_gain, w_in, q_gain, k_gain, rel_table, conv_w, w_o_att, w_o_conv, w_o, w_pq, sub_keys1, sub_keys2, expert_u, expert_v):
    depth = w_ada.shape[0]
    nbp, seq_p, _ = x_prompt.shape
    nbs, seq_s, _ = x_sample.shape
    xp, xs = x_prompt, x_sample
    c_all = jnp.concatenate([c_prompt, c_sample], axis=0)
    outs = [[] for _ in range(6)]
    for l in range(depth):
        w = _layer_weights(l, norm1_gain, norm2_gain, w_in, q_gain, k_gain, rel_table, conv_w,
                           w_o_att, w_o_conv, w_o, w_pq, sub_keys1, sub_keys2, expert_u, expert_v)
        mod = _adaln(c_all, w_ada[l], b_ada[l]).reshape(nbp + nbs, 6, D_MODEL)
        mod_p, mod_s = mod[:nbp], mod[nbp:]

        zeros_ctx = jnp.zeros((nbp, LEFT_CTX, ATT_DIM), BF16)
        zeros_conv = jnp.zeros((nbp, CONV_WIDTH - 1, CONV_DIM), F32)
        xp_mid, h2p, kp, vp, cp = _mixer(xp, mod_p, zeros_ctx, zeros_ctx, zeros_conv, w,
                                         tm=MIXER_TILE, mask_pad=True)
        ck = cache_k[l].reshape(nbs, -1, ATT_DIM).astype(BF16)
        cv = cache_v[l].reshape(nbs, -1, ATT_DIM).astype(BF16)
        xs_mid, h2s, kn, vn, cn = _mixer(xs, mod_s, ck, cv, cache_conv[l], w,
                                         tm=seq_s, mask_pad=False)

        xp = _peer(xp_mid, h2p, mod_p, w, nb=1, ts=PEER_TILE)
        xs = _peer(xs_mid, h2s, mod_s, w, nb=PEER_TILE // seq_s, ts=seq_s)

        for lst, val in zip(outs, (kp, vp, cp, kn, vn, cn)):
            lst.append(val)
    kp, vp, cp, kn, vn, cn = (jnp.stack(v) for v in outs)
    heads = (ATT_HEADS, HEAD_DIM)
    return (xp, xs,
            kp.reshape(kp.shape[:3] + heads), vp.reshape(vp.shape[:3] + heads), cp,
            kn.reshape(kn.shape[:3] + heads), vn.reshape(vn.shape[:3] + heads), cn)
```

```python
import functools

import jax
import jax.numpy as jnp
import numpy as np
from jax import lax
from jax.experimental import pallas as pl
from jax.experimental.pallas import tpu as pltpu

D_MODEL = 1024
CHUNK = 64
LEFT_CTX = 512
ATT_HEADS = 8
HEAD_DIM = 64
ATT_DIM = ATT_HEADS * HEAD_DIM
REL_CLIP = 128
CONV_DIM = 512
CONV_WIDTH = 3
PEER_HEADS = 8
PEER_NKEYS = 128
PEER_N = PEER_NKEYS * PEER_NKEYS
PEER_HALF = 128
PEER_TOPK = 16
EPS = 1e-6
IN_COLS = 3 * ATT_DIM + 3 * CONV_DIM + 2 * D_MODEL

LANES = 128
MIXER_TILE = 256
PEER_TILE = 512
PEER_EXPERT_BLOCK = 1024
VMEM_LIMIT = 56 * 1024 * 1024
NEG_BIG = -3.0e38
MASKED = -1e30

F32 = jnp.float32
BF16 = jnp.bfloat16
_NT = (((1,), (1,)), ((), ()))


def _const_spec(shape):
    zeros = (0,) * len(shape)
    return pl.BlockSpec(shape, lambda *_: zeros, pipeline_mode=pl.Buffered(1))


def _adaln_kernel(c_ref, w_ref, b_ref, o_ref):
    c = c_ref[...]
    s = c * jax.nn.sigmoid(c)
    o_ref[...] = jnp.dot(s.astype(BF16), w_ref[...].astype(BF16),
                         preferred_element_type=F32) + b_ref[...]


def _adaln(c_all, w_ada, b_ada):
    n = c_all.shape[0]
    cols = w_ada.shape[1]
    blk = 1024
    return pl.pallas_call(
        _adaln_kernel,
        grid=(cols // blk,),
        in_specs=[pl.BlockSpec((n, D_MODEL), lambda i: (0, 0)),
                  pl.BlockSpec((D_MODEL, blk), lambda i: (0, i)),
                  pl.BlockSpec((1, blk), lambda i: (0, i))],
        out_specs=pl.BlockSpec((n, blk), lambda i: (0, i)),
        out_shape=jax.ShapeDtypeStruct((n, cols), F32),
        compiler_params=pltpu.CompilerParams(dimension_semantics=("parallel",)),
        name="adaln",
    )(c_all, w_ada, b_ada.reshape(1, cols))


def _rms(x, gain_row):
    ms = jnp.mean(x * x, axis=-1, keepdims=True)
    return x * lax.rsqrt(ms + EPS) * gain_row


def _head_rms(t, seg, gain_row):
    sq = t * t
    hi = sq.astype(BF16)
    lo = (sq - hi.astype(F32)).astype(BF16)
    ms = (jnp.dot(hi, seg, preferred_element_type=F32)
          + jnp.dot(lo, seg, preferred_element_type=F32))
    return t * lax.rsqrt(ms + EPS) * gain_row


def _mixer_kernel(x_ref, mod_ref, ctxk_ref, ctxv_ref, cbuf_ref, g1_ref, win_ref, qg_ref, kg_ref,
                  seg_ref, bias_ref, cw_ref, woa_ref, woc_ref, wo_ref, g2_ref,
                  xmid_ref, h2_ref, kout_ref, vout_ref, cstate_ref,
                  kbuf, vbuf, ubuf, att_scr, *, tm, mask_pad):
    j = pl.program_id(1)
    band = LEFT_CTX + tm

    @pl.when(j == 0)
    def _():
        kbuf[0:LEFT_CTX, :] = ctxk_ref[0]
        vbuf[0:LEFT_CTX, :] = ctxv_ref[0]
        ubuf[0:8, :] = jnp.zeros((8, CONV_DIM), F32)
        ubuf[6:8, :] = cbuf_ref[0]

    x = x_ref[0]
    mod = mod_ref[0]
    shift1, scale1, gate1 = mod[0:1], mod[1:2], mod[2:3]
    shift2, scale2 = mod[3:4], mod[4:5]

    h = _rms(x, g1_ref[...]) * (1.0 + scale1) + shift1
    hb = h.astype(BF16)

    zq = jnp.dot(hb, win_ref[:, 0:3 * ATT_DIM], preferred_element_type=F32)
    seg = seg_ref[...]
    qn = _head_rms(zq[:, 0:ATT_DIM], seg, qg_ref[...])
    kn = _head_rms(zq[:, ATT_DIM:2 * ATT_DIM], seg, kg_ref[...])
    vv = zq[:, 2 * ATT_DIM:3 * ATT_DIM]
    kout_ref[0] = kn
    vout_ref[0] = vv
    kbuf[LEFT_CTX:band, :] = kn.astype(BF16)
    vbuf[LEFT_CTX:band, :] = vv.astype(BF16)
    qs = (qn * (HEAD_DIM ** -0.5)).astype(BF16)

    if mask_pad:
        col = lax.broadcasted_iota(jnp.int32, (tm, band), 1)
        valid = col >= (LEFT_CTX - j * tm)
    for hd in range(ATT_HEADS):
        hs = slice(hd * HEAD_DIM, (hd + 1) * HEAD_DIM)
        s = lax.dot_general(qs[:, hs], kbuf[:, hs], _NT, preferred_element_type=F32)
        s = s + bias_ref[hd]
        if mask_pad:
            s = jnp.where(valid, s, MASKED)
        m = jnp.max(s, axis=-1, keepdims=True)
        p = jnp.exp(s - m)
        l = jnp.sum(p, axis=-1, keepdims=True)
        o = jnp.dot(p.astype(BF16), vbuf[:, hs], preferred_element_type=F32)
        att_scr[:, hs] = o / l
    y_att = jnp.dot(att_scr[...].astype(BF16), woa_ref[...], preferred_element_type=F32)

    kbuf[0:LEFT_CTX, :] = kbuf[tm:tm + LEFT_CTX, :]
    vbuf[0:LEFT_CTX, :] = vbuf[tm:tm + LEFT_CTX, :]

    zc = jnp.dot(hb, win_ref[:, 3 * ATT_DIM:3 * ATT_DIM + 3 * CONV_DIM], preferred_element_type=F32)
    cb = zc[:, 0:CONV_DIM]
    ubuf[8:8 + tm, :] = zc[:, CONV_DIM:2 * CONV_DIM] * zc[:, 2 * CONV_DIM:3 * CONV_DIM]
    cw = cw_ref[...]
    y = ubuf[6:6 + tm, :] * cw[0:1]
    y = y + ubuf[7:7 + tm, :] * cw[1:2]
    y = y + ubuf[8:8 + tm, :] * cw[2:3]
    tail = ubuf[6 + tm:8 + tm, :]
    cstate_ref[0] = tail
    ubuf[6:8, :] = tail
    y_conv = jnp.dot((cb * y).astype(BF16), woc_ref[...], preferred_element_type=F32)

    zg = jnp.dot(hb, win_ref[:, 3 * ATT_DIM + 3 * CONV_DIM:IN_COLS], preferred_element_type=F32)
    merged = (jax.nn.sigmoid(zg[:, 0:D_MODEL]) * y_att
              + jax.nn.sigmoid(zg[:, D_MODEL:2 * D_MODEL]) * y_conv)
    mixed = jnp.dot(merged.astype(BF16), wo_ref[...], preferred_element_type=F32)
    x1 = x + gate1 * mixed
    xmid_ref[0] = x1
    h2 = _rms(x1, g2_ref[...]) * (1.0 + scale2) + shift2
    h2_ref[0] = h2.astype(BF16)


def _rel_bias(rel_table, tm):
    band = LEFT_CTX + tm
    period = band + tm - 1
    n_far_past = LEFT_CTX - REL_CLIP + tm - 1
    n_future = max(period - n_far_past - (2 * REL_CLIP + 1), 0)
    nh = rel_table.shape[0]
    by_offset = jnp.concatenate([
        jnp.broadcast_to(rel_table[:, -1:], (nh, n_far_past)),
        rel_table[:, ::-1],
        jnp.broadcast_to(rel_table[:, :1], (nh, n_future))], axis=1)[:, :period]
    skew = jnp.tile(by_offset, (1, tm + 1))[:, :tm * (period + 1)].reshape(nh, tm, period + 1)
    bias = skew[:, ::-1, :band]
    qi = np.arange(tm)[:, None]
    kj = np.arange(band)[None, :]
    lo = (qi // CHUNK) * CHUNK
    allowed = (kj >= lo) & (kj < lo + LEFT_CTX + CHUNK)
    return jnp.where(allowed[None], bias, MASKED).astype(F32)


def _mixer(x, mod, ctx_k, ctx_v, conv_buf, w, *, tm, mask_pad):
    nb, seq, _ = x.shape
    n_steps = seq // tm
    n_keep = min(LEFT_CTX, seq)
    first_kept = n_steps - n_keep // tm
    band = LEFT_CTX + tm
    bias = _rel_bias(w["rel_table"], tm)

    tile = lambda b, j: (b, j, 0)
    per_row = lambda b, j: (b, 0, 0)
    kept = lambda b, j: (b, jnp.maximum(j - first_kept, 0), 0)
    in_specs = [
        pl.BlockSpec((1, tm, D_MODEL), tile),
        pl.BlockSpec((1, 6, D_MODEL), per_row),
        pl.BlockSpec((1, LEFT_CTX, ATT_DIM), per_row),
        pl.BlockSpec((1, LEFT_CTX, ATT_DIM), per_row),
        pl.BlockSpec((1, CONV_WIDTH - 1, CONV_DIM), per_row),
        _const_spec((1, D_MODEL)),
        _const_spec((D_MODEL, IN_COLS)),
        _const_spec((1, ATT_DIM)),
        _const_spec((1, ATT_DIM)),
        _const_spec((ATT_DIM, ATT_DIM)),
        _const_spec((ATT_HEADS, tm, band)),
        _const_spec((CONV_WIDTH, CONV_DIM)),
        _const_spec((ATT_DIM, D_MODEL)),
        _const_spec((CONV_DIM, D_MODEL)),
        _const_spec((D_MODEL, D_MODEL)),
        _const_spec((1, D_MODEL)),
    ]
    out_specs = [
        pl.BlockSpec((1, tm, D_MODEL), tile),
        pl.BlockSpec((1, tm, D_MODEL), tile),
        pl.BlockSpec((1, tm, ATT_DIM), kept),
        pl.BlockSpec((1, tm, ATT_DIM), kept),
        pl.BlockSpec((1, CONV_WIDTH - 1, CONV_DIM), per_row),
    ]
    out_shape = [
        jax.ShapeDtypeStruct((nb, seq, D_MODEL), F32),
        jax.ShapeDtypeStruct((nb, seq, D_MODEL), BF16),
        jax.ShapeDtypeStruct((nb, n_keep, ATT_DIM), F32),
        jax.ShapeDtypeStruct((nb, n_keep, ATT_DIM), F32),
        jax.ShapeDtypeStruct((nb, CONV_WIDTH - 1, CONV_DIM), F32),
    ]
    scratch = [
        pltpu.VMEM((band, ATT_DIM), BF16),
        pltpu.VMEM((band, ATT_DIM), BF16),
        pltpu.VMEM((8 + tm, CONV_DIM), F32),
        pltpu.VMEM((tm, ATT_DIM), F32),
    ]
    return pl.pallas_call(
        functools.partial(_mixer_kernel, tm=tm, mask_pad=mask_pad),
        grid=(nb, n_steps),
        in_specs=in_specs,
        out_specs=out_specs,
        out_shape=out_shape,
        scratch_shapes=scratch,
        compiler_params=pltpu.CompilerParams(
            dimension_semantics=("parallel", "arbitrary"), vmem_limit_bytes=VMEM_LIMIT),
        name="mixer",
    )(x, mod, ctx_k, ctx_v, conv_buf, w["g1"], w["w_in"], w["q_gain"], w["k_gain"], w["seg"],
      bias, w["conv_w"], w["w_o_att"], w["w_o_conv"], w["w_o"], w["g2"])


def _gelu_tanh(x):
    k0 = -2.0 * 1.4426950408889634 * 0.7978845608028654
    k1 = k0 * 0.044715
    return x / (1.0 + jnp.exp2(x * (k0 + k1 * (x * x))))


def _top_values(cur, dst_ref, lanes, n):
    rank = jnp.full(cur.shape, float(n), F32)
    for r in range(n):
        m = jnp.max(cur, axis=0, keepdims=True)
        dst_ref[r:r + 1, lanes] = m
        hit = cur == m
        rank = jnp.where(hit, float(r), rank)
        cur = jnp.where(hit, NEG_BIG, cur)
    return rank


_CAND_COUNTS = tuple(PEER_TOPK // (r1 + 1) for r1 in range(PEER_TOPK))
_N_CAND = sum(_CAND_COUNTS)
_CAND_ROWS = -(-_N_CAND // 8) * 8


def _pair_bf16(x):
    hi = pltpu.bitcast(x.astype(BF16).astype(F32), jnp.uint32)
    return pltpu.bitcast(hi | lax.shift_right_logical(hi, jnp.uint32(16)), F32)


GATE_ROWS = 64
CHUNK_KEYS = 2


def _bcast_rows_bf16(pair_row):
    return pltpu.bitcast(jnp.broadcast_to(pair_row, (GATE_ROWS // 2, LANES)), BF16)


def _peer_kernel(xmid_ref, h2_ref, mod_ref, wpq_ref, k1_ref, k2_ref, u_ref, vt_ref, out_ref,
                 h2s, h2t, s1_scr, s2_scr, pk_scr, rb_scr, e2_scr, v1_scr, v2_scr, cand_scr,
                 p_scr, acc_scr, *, nb, ts):
    j = pl.program_id(1)
    tm = nb * ts
    n_lane_tiles = tm // LANES
    chunks_per_step = u_ref.shape[0]

    @pl.when(j == 0)
    def _retrieve():
        for b in range(nb):
            h2s[b * ts:(b + 1) * ts, :] = h2_ref[b]
        acc_scr[...] = jnp.zeros(acc_scr.shape, F32)
        h2v = h2s[...]
        h2t[...] = h2v.astype(F32).T.astype(BF16)

        def per_head(hd, carry):
            col = pl.multiple_of(hd * (2 * PEER_HALF), 2 * PEER_HALF)
            q = jnp.dot(h2v, wpq_ref[:, pl.ds(col, 2 * PEER_HALF)], preferred_element_type=F32)
            q1 = q[:, 0:PEER_HALF].astype(BF16)
            q2 = q[:, PEER_HALF:2 * PEER_HALF].astype(BF16)
            s1_scr[...] = lax.dot_general(k1_ref[hd], q1, _NT, preferred_element_type=F32)
            s2_scr[...] = lax.dot_general(k2_ref[hd], q2, _NT, preferred_element_type=F32)
            for lt in range(n_lane_tiles):
                lanes = slice(lt * LANES, (lt + 1) * LANES)
                s1 = s1_scr[:, lanes]
                s2 = s2_scr[:, lanes]
                _top_values(s1, v1_scr, lanes, PEER_TOPK)
                rank2 = _top_values(s2, v2_scr, lanes, PEER_TOPK)
                v1 = v1_scr[:, lanes]
                v2 = v2_scr[:, lanes]
                off = 0
                for r1, cnt in enumerate(_CAND_COUNTS):
                    cand_scr[off:off + cnt, lanes] = v1[r1:r1 + 1] + v2[0:cnt]
                    off += cnt
                if _CAND_ROWS > _N_CAND:
                    cand_scr[_N_CAND:_CAND_ROWS, lanes] = jnp.full(
                        (_CAND_ROWS - _N_CAND, LANES), NEG_BIG, F32)
                cur = cand_scr[:, lanes]
                top = v1[0:1] + v2[0:1]
                z = jnp.zeros((1, LANES), F32)
                tau = top
                for _ in range(PEER_TOPK):
                    tau = jnp.max(cur, axis=0, keepdims=True)
                    z = z + jnp.exp(tau - top)
                    cur = jnp.where(cur == tau, NEG_BIG, cur)
                count = jnp.zeros((PEER_NKEYS, LANES), F32)
                for r in range(PEER_TOPK):
                    v1r = v1[r:r + 1]
                    n_r = jnp.sum(jnp.where(v1r + v2 >= tau, 1.0, 0.0), axis=0, keepdims=True)
                    count = jnp.where(s1 == v1r, n_r, count)
                count_pk = _pair_bf16(count)
                coef_pk = _pair_bf16(jnp.exp(s1 - v1[0:1]) / z)
                for c in range(PEER_NKEYS // CHUNK_KEYS):
                    keys = slice(c * CHUNK_KEYS, (c + 1) * CHUNK_KEYS)
                    pk_scr[hd, c, 0:CHUNK_KEYS, lanes] = count_pk[keys]
                    pk_scr[hd, c, CHUNK_KEYS:2 * CHUNK_KEYS, lanes] = coef_pk[keys]
                rb_scr[hd, :, lanes] = rank2.astype(BF16)
                e2_scr[hd, :, lanes] = jnp.exp(s2 - v2[0:1]).astype(BF16)
            return carry

        lax.fori_loop(0, PEER_HEADS, per_head, 0)

    def pre_activations(c):
        return jnp.dot(u_ref[c], h2t[...], preferred_element_type=F32)

    def gate(c, act_all):
        chunk = j * chunks_per_step + c
        for lt in range(n_lane_tiles):
            lanes = slice(lt * LANES, (lt + 1) * LANES)
            for part in range(PEER_NKEYS // GATE_ROWS):
                keys = slice(part * GATE_ROWS, (part + 1) * GATE_ROWS)
                ws = [jnp.zeros((GATE_ROWS, LANES), BF16) for _ in range(CHUNK_KEYS)]
                for hd in range(PEER_HEADS):
                    pk = pk_scr[hd, chunk, 0:2 * CHUNK_KEYS, lanes]
                    rank2 = rb_scr[hd, keys, lanes]
                    e2 = e2_scr[hd, keys, lanes]
                    for g in range(CHUNK_KEYS):
                        count = _bcast_rows_bf16(pk[g:g + 1])
                        coef = _bcast_rows_bf16(pk[CHUNK_KEYS + g:CHUNK_KEYS + g + 1])
                        ws[g] = ws[g] + jnp.where(rank2 < count, e2, jnp.zeros_like(e2)) * coef
                for g in range(CHUNK_KEYS):
                    first = g * PEER_NKEYS + part * GATE_ROWS
                    rows = slice(first, first + GATE_ROWS)
                    act = _gelu_tanh(act_all[rows, lanes])
                    p_scr[c % 2, rows, lanes] = (ws[g].astype(F32) * act).astype(BF16)

    def accumulate(c):
        acc_scr[...] += jnp.dot(vt_ref[c], p_scr[c % 2], preferred_element_type=F32)

    act_next = pre_activations(0)
    for c in range(chunks_per_step):
        act_cur = act_next
        if c + 1 < chunks_per_step:
            act_next = pre_activations(c + 1)
        gate(c, act_cur)
        accumulate(c)

    @pl.when(j == pl.num_programs(1) - 1)
    def _finish():
        peer = acc_scr[...].T
        mod = mod_ref[...]
        for b in range(nb):
            gate2 = mod[b, 5:6, :]
            out_ref[b] = xmid_ref[b] + gate2 * peer[b * ts:(b + 1) * ts, :]


def _peer(x_mid, h2, mod, w, *, nb, ts):
    n_rows, seq, _ = x_mid.shape
    tm = nb * ts
    tiles_per_row = seq // ts
    n_tiles = (n_rows // nb) * tiles_per_row
    chunk = CHUNK_KEYS * PEER_NKEYS
    n_chunks = PEER_N // chunk
    chunks_per_step = PEER_EXPERT_BLOCK // chunk
    n_steps = PEER_N // PEER_EXPERT_BLOCK

    tok = lambda i, j: (i // tiles_per_row, i % tiles_per_row, 0)
    row = lambda i, j: (i // tiles_per_row, 0, 0)
    in_specs = [
        pl.BlockSpec((nb, ts, D_MODEL), tok),
        pl.BlockSpec((nb, ts, D_MODEL), tok),
        pl.BlockSpec((nb, 6, D_MODEL), row),
        _const_spec((D_MODEL, PEER_HEADS * 2 * PEER_HALF)),
        _const_spec((PEER_HEADS, PEER_NKEYS, PEER_HALF)),
        _const_spec((PEER_HEADS, PEER_NKEYS, PEER_HALF)),
        pl.BlockSpec((chunks_per_step, chunk, D_MODEL), lambda i, j: (j, 0, 0)),
        pl.BlockSpec((chunks_per_step, D_MODEL, chunk), lambda i, j: (j, 0, 0)),
    ]
    per_head = (PEER_HEADS, PEER_NKEYS, tm)
    scratch = [
        pltpu.VMEM((tm, D_MODEL), BF16),
        pltpu.VMEM((D_MODEL, tm), BF16),
        pltpu.VMEM((PEER_NKEYS, tm), F32),
        pltpu.VMEM((PEER_NKEYS, tm), F32),
        pltpu.VMEM((PEER_HEADS, n_chunks, 8, tm), F32),
        pltpu.VMEM(per_head, BF16),
        pltpu.VMEM(per_head, BF16),
        pltpu.VMEM((PEER_TOPK, tm), F32),
        pltpu.VMEM((PEER_TOPK, tm), F32),
        pltpu.VMEM((_CAND_ROWS, tm), F32),
        pltpu.VMEM((2, chunk, tm), BF16),
        pltpu.VMEM((D_MODEL, tm), F32),
    ]
    return pl.pallas_call(
        functools.partial(_peer_kernel, nb=nb, ts=ts),
        grid=(n_tiles, n_steps),
        in_specs=in_specs,
        out_specs=pl.BlockSpec((nb, ts, D_MODEL), tok),
        out_shape=jax.ShapeDtypeStruct(x_mid.shape, F32),
        scratch_shapes=scratch,
        compiler_params=pltpu.CompilerParams(
            dimension_semantics=("parallel", "arbitrary"), vmem_limit_bytes=VMEM_LIMIT),
        name="peer",
    )(x_mid, h2, mod, w["w_pq"], w["sub_keys1"], w["sub_keys2"], w["expert_u"], w["expert_vt"])


def _layer_weights(l, norm1_gain, norm2_gain, w_in, q_gain, k_gain, rel_table, conv_w, w_o_att,
                   w_o_conv, w_o, w_pq, sub_keys1, sub_keys2, expert_u, expert_v):
    head_of = jnp.arange(ATT_DIM) // HEAD_DIM
    seg = jnp.where(head_of[:, None] == head_of[None, :], 1.0 / HEAD_DIM, 0.0).astype(BF16)
    return {
        "g1": norm1_gain[l].reshape(1, D_MODEL),
        "g2": norm2_gain[l].reshape(1, D_MODEL),
        "w_in": w_in[l].astype(BF16),
        "q_gain": jnp.tile(q_gain[l], ATT_HEADS).reshape(1, ATT_DIM),
        "k_gain": jnp.tile(k_gain[l], ATT_HEADS).reshape(1, ATT_DIM),
        "seg": seg,
        "rel_table": rel_table[l],
        "conv_w": conv_w[l],
        "w_o_att": w_o_att[l].astype(BF16),
        "w_o_conv": w_o_conv[l].astype(BF16),
        "w_o": w_o[l].astype(BF16),
        "w_pq": w_pq[l].astype(BF16),
        "sub_keys1": sub_keys1[l].astype(BF16),
        "sub_keys2": sub_keys2[l].astype(BF16),
        "expert_u": expert_u[l].astype(BF16).reshape(-1, CHUNK_KEYS * PEER_NKEYS, D_MODEL),
        "expert_vt": expert_v[l].astype(BF16).reshape(-1, CHUNK_KEYS * PEER_NKEYS, D_MODEL).transpose(0, 2, 1),
    }


def kernel(x_prompt, c_prompt, x_sample, c_sample, cache_k, cache_v, cache_conv, w_ada, b_ada, norm1_gain, norm2_gain, w_in, q_gain, k_gain, rel_table, conv_w, w_o_att, w_o_conv, w_o, w_pq, sub_keys1, sub_keys2, expert_u, expert_v):
    depth = w_ada.shape[0]
    nbp, seq_p, _ = x_prompt.shape
    nbs, seq_s, _ = x_sample.shape
    xp, xs = x_prompt, x_sample
    c_all = jnp.concatenate([c_prompt, c_sample], axis=0)
    outs = [[] for _ in range(6)]
    for l in range(depth):
        w = _layer_weights(l, norm1_gain, norm2_gain, w_in, q_gain, k_gain, rel_table, conv_w,
                           w_o_att, w_o_conv, w_o, w_pq, sub_keys1, sub_keys2, expert_u, expert_v)
        mod = _adaln(c_all, w_ada[l], b_ada[l]).reshape(nbp + nbs, 6, D_MODEL)
        mod_p, mod_s = mod[:nbp], mod[nbp:]

        zeros_ctx = jnp.zeros((nbp, LEFT_CTX, ATT_DIM), BF16)
        zeros_conv = jnp.zeros((nbp, CONV_WIDTH - 1, CONV_DIM), F32)
        xp_mid, h2p, kp, vp, cp = _mixer(xp, mod_p, zeros_ctx, zeros_ctx, zeros_conv, w,
                                         tm=MIXER_TILE, mask_pad=True)
        ck = cache_k[l].reshape(nbs, -1, ATT_DIM).astype(BF16)
        cv = cache_v[l].reshape(nbs, -1, ATT_DIM).astype(BF16)
        xs_mid, h2s, kn, vn, cn = _mixer(xs, mod_s, ck, cv, cache_conv[l], w,
                                         tm=seq_s, mask_pad=False)

        xp = _peer(xp_mid, h2p, mod_p, w, nb=1, ts=PEER_TILE)
        xs = _peer(xs_mid, h2s, mod_s, w, nb=PEER_TILE // seq_s, ts=seq_s)

        for lst, val in zip(outs, (kp, vp, cp, kn, vn, cn)):
            lst.append(val)
    kp, vp, cp, kn, vn, cn = (jnp.stack(v) for v in outs)
    heads = (ATT_HEADS, HEAD_DIM)
    return (xp, xs,
            kp.reshape(kp.shape[:3] + heads), vp.reshape(vp.shape[:3] + heads), cp,
            kn.reshape(kn.shape[:3] + heads), vn.reshape(vn.shape[:3] + heads), cn)
```

```python
import functools

import jax
import jax.numpy as jnp
import numpy as np
from jax import lax
from jax.experimental import pallas as pl
from jax.experimental.pallas import tpu as pltpu

D_MODEL = 1024
CHUNK = 64
LEFT_CTX = 512
ATT_HEADS = 8
HEAD_DIM = 64
ATT_DIM = ATT_HEADS * HEAD_DIM
REL_CLIP = 128
CONV_DIM = 512
CONV_WIDTH = 3
PEER_HEADS = 8
PEER_NKEYS = 128
PEER_N = PEER_NKEYS * PEER_NKEYS
PEER_HALF = 128
PEER_TOPK = 16
EPS = 1e-6
IN_COLS = 3 * ATT_DIM + 3 * CONV_DIM + 2 * D_MODEL

LANES = 128
MIXER_TILE = 256
PEER_TILE = 512
PEER_EXPERT_BLOCK = 2048
VMEM_LIMIT = 56 * 1024 * 1024
NEG_BIG = -3.0e38
MASKED = -1e30

F32 = jnp.float32
BF16 = jnp.bfloat16
_NT = (((1,), (1,)), ((), ()))


def _const_spec(shape):
    zeros = (0,) * len(shape)
    return pl.BlockSpec(shape, lambda *_: zeros, pipeline_mode=pl.Buffered(1))


def _adaln_kernel(c_ref, w_ref, b_ref, o_ref):
    c = c_ref[...]
    s = c * jax.nn.sigmoid(c)
    o_ref[...] = jnp.dot(s.astype(BF16), w_ref[...].astype(BF16),
                         preferred_element_type=F32) + b_ref[...]


def _adaln(c_all, w_ada, b_ada):
    n = c_all.shape[0]
    cols = w_ada.shape[1]
    blk = 1024
    return pl.pallas_call(
        _adaln_kernel,
        grid=(cols // blk,),
        in_specs=[pl.BlockSpec((n, D_MODEL), lambda i: (0, 0)),
                  pl.BlockSpec((D_MODEL, blk), lambda i: (0, i)),
                  pl.BlockSpec((1, blk), lambda i: (0, i))],
        out_specs=pl.BlockSpec((n, blk), lambda i: (0, i)),
        out_shape=jax.ShapeDtypeStruct((n, cols), F32),
        compiler_params=pltpu.CompilerParams(dimension_semantics=("parallel",)),
        name="adaln",
    )(c_all, w_ada, b_ada.reshape(1, cols))


def _rms(x, gain_row):
    ms = jnp.mean(x * x, axis=-1, keepdims=True)
    return x * lax.rsqrt(ms + EPS) * gain_row


def _head_rms(t, seg, gain_row):
    sq = t * t
    hi = sq.astype(BF16)
    lo = (sq - hi.astype(F32)).astype(BF16)
    ms = (jnp.dot(hi, seg, preferred_element_type=F32)
          + jnp.dot(lo, seg, preferred_element_type=F32))
    return t * lax.rsqrt(ms + EPS) * gain_row


def _mixer_kernel(x_ref, mod_ref, ctxk_ref, ctxv_ref, cbuf_ref, g1_ref, win_ref, qg_ref, kg_ref,
                  seg_ref, bias_ref, cw_ref, woa_ref, woc_ref, wo_ref, g2_ref,
                  xmid_ref, h2_ref, kout_ref, vout_ref, cstate_ref,
                  kbuf, vbuf, ubuf, att_scr, *, tm, mask_pad):
    j = pl.program_id(1)
    band = LEFT_CTX + tm

    @pl.when(j == 0)
    def _():
        kbuf[0:LEFT_CTX, :] = ctxk_ref[0]
        vbuf[0:LEFT_CTX, :] = ctxv_ref[0]
        ubuf[0:8, :] = jnp.zeros((8, CONV_DIM), F32)
        ubuf[6:8, :] = cbuf_ref[0]

    x = x_ref[0]
    mod = mod_ref[0]
    shift1, scale1, gate1 = mod[0:1], mod[1:2], mod[2:3]
    shift2, scale2 = mod[3:4], mod[4:5]

    h = _rms(x, g1_ref[...]) * (1.0 + scale1) + shift1
    hb = h.astype(BF16)

    zq = jnp.dot(hb, win_ref[:, 0:3 * ATT_DIM], preferred_element_type=F32)
    seg = seg_ref[...]
    qn = _head_rms(zq[:, 0:ATT_DIM], seg, qg_ref[...])
    kn = _head_rms(zq[:, ATT_DIM:2 * ATT_DIM], seg, kg_ref[...])
    vv = zq[:, 2 * ATT_DIM:3 * ATT_DIM]
    kout_ref[0] = kn
    vout_ref[0] = vv
    kbuf[LEFT_CTX:band, :] = kn.astype(BF16)
    vbuf[LEFT_CTX:band, :] = vv.astype(BF16)
    qs = (qn * (HEAD_DIM ** -0.5)).astype(BF16)

    if mask_pad:
        col = lax.broadcasted_iota(jnp.int32, (tm, band), 1)
        valid = col >= (LEFT_CTX - j * tm)
    for hd in range(ATT_HEADS):
        hs = slice(hd * HEAD_DIM, (hd + 1) * HEAD_DIM)
        s = lax.dot_general(qs[:, hs], kbuf[:, hs], _NT, preferred_element_type=F32)
        s = s + bias_ref[hd]
        if mask_pad:
            s = jnp.where(valid, s, MASKED)
        m = jnp.max(s, axis=-1, keepdims=True)
        p = jnp.exp(s - m)
        l = jnp.sum(p, axis=-1, keepdims=True)
        o = jnp.dot(p.astype(BF16), vbuf[:, hs], preferred_element_type=F32)
        att_scr[:, hs] = o / l
    y_att = jnp.dot(att_scr[...].astype(BF16), woa_ref[...], preferred_element_type=F32)

    kbuf[0:LEFT_CTX, :] = kbuf[tm:tm + LEFT_CTX, :]
    vbuf[0:LEFT_CTX, :] = vbuf[tm:tm + LEFT_CTX, :]

    zc = jnp.dot(hb, win_ref[:, 3 * ATT_DIM:3 * ATT_DIM + 3 * CONV_DIM], preferred_element_type=F32)
    cb = zc[:, 0:CONV_DIM]
    ubuf[8:8 + tm, :] = zc[:, CONV_DIM:2 * CONV_DIM] * zc[:, 2 * CONV_DIM:3 * CONV_DIM]
    cw = cw_ref[...]
    y = ubuf[6:6 + tm, :] * cw[0:1]
    y = y + ubuf[7:7 + tm, :] * cw[1:2]
    y = y + ubuf[8:8 + tm, :] * cw[2:3]
    tail = ubuf[6 + tm:8 + tm, :]
    cstate_ref[0] = tail
    ubuf[6:8, :] = tail
    y_conv = jnp.dot((cb * y).astype(BF16), woc_ref[...], preferred_element_type=F32)

    zg = jnp.dot(hb, win_ref[:, 3 * ATT_DIM + 3 * CONV_DIM:IN_COLS], preferred_element_type=F32)
    merged = (jax.nn.sigmoid(zg[:, 0:D_MODEL]) * y_att
              + jax.nn.sigmoid(zg[:, D_MODEL:2 * D_MODEL]) * y_conv)
    mixed = jnp.dot(merged.astype(BF16), wo_ref[...], preferred_element_type=F32)
    x1 = x + gate1 * mixed
    xmid_ref[0] = x1
    h2 = _rms(x1, g2_ref[...]) * (1.0 + scale2) + shift2
    h2_ref[0] = h2.astype(BF16)


def _rel_bias(rel_table, tm):
    band = LEFT_CTX + tm
    period = band + tm - 1
    n_far_past = LEFT_CTX - REL_CLIP
    n_future = max(band - n_far_past - (2 * REL_CLIP + 1), 0)
    nh = rel_table.shape[0]
    by_offset = jnp.concatenate([
        jnp.concatenate([
            jnp.broadcast_to(rel_table[:, -1:], (nh, n_far_past)),
            rel_table[:, ::-1],
            jnp.broadcast_to(rel_table[:, :1], (nh, n_future))], axis=1)[:, :band],
        jnp.broadcast_to(rel_table[:, -1:], (nh, tm - 1))], axis=1)
    skew = jnp.tile(by_offset, (1, tm))[:, :tm * (period - 1)].reshape(nh, tm, period - 1)
    bias = skew[:, :, :band]
    qi = np.arange(tm)[:, None]
    kj = np.arange(band)[None, :]
    lo = (qi // CHUNK) * CHUNK
    allowed = (kj >= lo) & (kj < lo + LEFT_CTX + CHUNK)
    return jnp.where(allowed[None], bias, MASKED).astype(F32)


def _mixer(x, mod, ctx_k, ctx_v, conv_buf, w, *, tm, mask_pad):
    nb, seq, _ = x.shape
    n_steps = seq // tm
    n_keep = min(LEFT_CTX, seq)
    first_kept = n_steps - n_keep // tm
    band = LEFT_CTX + tm
    bias = _rel_bias(w["rel_table"], tm)

    tile = lambda b, j: (b, j, 0)
    per_row = lambda b, j: (b, 0, 0)
    kept = lambda b, j: (b, jnp.maximum(j - first_kept, 0), 0)
    in_specs = [
        pl.BlockSpec((1, tm, D_MODEL), tile),
        pl.BlockSpec((1, 6, D_MODEL), per_row),
        pl.BlockSpec((1, LEFT_CTX, ATT_DIM), per_row),
        pl.BlockSpec((1, LEFT_CTX, ATT_DIM), per_row),
        pl.BlockSpec((1, CONV_WIDTH - 1, CONV_DIM), per_row),
        _const_spec((1, D_MODEL)),
        _const_spec((D_MODEL, IN_COLS)),
        _const_spec((1, ATT_DIM)),
        _const_spec((1, ATT_DIM)),
        _const_spec((ATT_DIM, ATT_DIM)),
        _const_spec((ATT_HEADS, tm, band)),
        _const_spec((CONV_WIDTH, CONV_DIM)),
        _const_spec((ATT_DIM, D_MODEL)),
        _const_spec((CONV_DIM, D_MODEL)),
        _const_spec((D_MODEL, D_MODEL)),
        _const_spec((1, D_MODEL)),
    ]
    out_specs = [
        pl.BlockSpec((1, tm, D_MODEL), tile),
        pl.BlockSpec((1, tm, D_MODEL), tile),
        pl.BlockSpec((1, tm, ATT_DIM), kept),
        pl.BlockSpec((1, tm, ATT_DIM), kept),
        pl.BlockSpec((1, CONV_WIDTH - 1, CONV_DIM), per_row),
    ]
    out_shape = [
        jax.ShapeDtypeStruct((nb, seq, D_MODEL), F32),
        jax.ShapeDtypeStruct((nb, seq, D_MODEL), BF16),
        jax.ShapeDtypeStruct((nb, n_keep, ATT_DIM), F32),
        jax.ShapeDtypeStruct((nb, n_keep, ATT_DIM), F32),
        jax.ShapeDtypeStruct((nb, CONV_WIDTH - 1, CONV_DIM), F32),
    ]
    scratch = [
        pltpu.VMEM((band, ATT_DIM), BF16),
        pltpu.VMEM((band, ATT_DIM), BF16),
        pltpu.VMEM((8 + tm, CONV_DIM), F32),
        pltpu.VMEM((tm, ATT_DIM), F32),
    ]
    return pl.pallas_call(
        functools.partial(_mixer_kernel, tm=tm, mask_pad=mask_pad),
        grid=(nb, n_steps),
        in_specs=in_specs,
        out_specs=out_specs,
        out_shape=out_shape,
        scratch_shapes=scratch,
        compiler_params=pltpu.CompilerParams(
            dimension_semantics=("parallel", "arbitrary"), vmem_limit_bytes=VMEM_LIMIT),
        name="mixer",
    )(x, mod, ctx_k, ctx_v, conv_buf, w["g1"], w["w_in"], w["q_gain"], w["k_gain"], w["seg"],
      bias, w["conv_w"], w["w_o_att"], w["w_o_conv"], w["w_o"], w["g2"])


def _gelu_tanh(x):
    k0 = -2.0 * 1.4426950408889634 * 0.7978845608028654
    k1 = k0 * 0.044715
    return x / (1.0 + jnp.exp2(x * (k0 + k1 * (x * x))))


def _sorting_network(n):
    pairs = []
    p = 1
    while p < n:
        k = p
        while k >= 1:
            for j in range(k % p, n - k, 2 * k):
                for i in range(min(k, n - j - k)):
                    if (i + j) // (2 * p) == (i + j + k) // (2 * p):
                        pairs.append((i + j, i + j + k))
            k //= 2
        p *= 2
    return tuple(pairs)


SUBLANES = 8


def _top_values(cur, dst_ref, lanes, n):
    lists = [cur[k * SUBLANES:(k + 1) * SUBLANES] for k in range(n)]
    for a, b in _sorting_network(n):
        lists[a], lists[b] = jnp.maximum(lists[a], lists[b]), jnp.minimum(lists[a], lists[b])
    for r in range(n):
        head = lists[0]
        m = jnp.max(head, axis=0, keepdims=True)
        dst_ref[r:r + 1, lanes] = m
        hit = head == m
        for k in range(n - 1 - r):
            lists[k] = jnp.where(hit, lists[k + 1], lists[k])


def _rank_among(x, tops, n):
    rank = jnp.full(x.shape, float(n), F32)
    for r in range(n - 1, -1, -1):
        rank = jnp.where(x >= tops[r:r + 1], float(r), rank)
    return rank


_CAND_COUNTS = tuple(PEER_TOPK // (r1 + 1) for r1 in range(PEER_TOPK))
_N_CAND = sum(_CAND_COUNTS)
_CAND_ROWS = -(-_N_CAND // 8) * 8


def _pair_bf16(x):
    hi = pltpu.bitcast(x.astype(BF16).astype(F32), jnp.uint32)
    return pltpu.bitcast(hi | lax.shift_right_logical(hi, jnp.uint32(16)), F32)


GATE_ROWS = 64
CHUNK_KEYS = 4


def _bcast_rows_bf16(pair_row):
    return pltpu.bitcast(jnp.broadcast_to(pair_row, (GATE_ROWS // 2, LANES)), BF16)


def _peer_kernel(xmid_ref, h2_ref, mod_ref, wpq_ref, k1_ref, k2_ref, u_ref, vt_ref, out_ref,
                 h2s, h2t, s1_scr, s2_scr, pk_scr, rb_scr, e2_scr, v1_scr, v2_scr, cand_scr,
                 p_scr, acc_scr, *, nb, ts):
    j = pl.program_id(1)
    tm = nb * ts
    n_lane_tiles = tm // LANES
    chunks_per_step = u_ref.shape[0]

    @pl.when(j == 0)
    def _retrieve():
        for b in range(nb):
            h2s[b * ts:(b + 1) * ts, :] = h2_ref[b]
        acc_scr[...] = jnp.zeros(acc_scr.shape, F32)
        h2v = h2s[...]
        h2t[...] = h2v.astype(F32).T.astype(BF16)

        def per_head(hd, carry):
            col = pl.multiple_of(hd * (2 * PEER_HALF), 2 * PEER_HALF)
            q = jnp.dot(h2v, wpq_ref[:, pl.ds(col, 2 * PEER_HALF)], preferred_element_type=F32)
            q1 = q[:, 0:PEER_HALF].astype(BF16)
            q2 = q[:, PEER_HALF:2 * PEER_HALF].astype(BF16)
            s1_scr[...] = lax.dot_general(k1_ref[hd], q1, _NT, preferred_element_type=F32)
            s2_scr[...] = lax.dot_general(k2_ref[hd], q2, _NT, preferred_element_type=F32)
            for lt in range(n_lane_tiles):
                lanes = slice(lt * LANES, (lt + 1) * LANES)
                s1 = s1_scr[:, lanes]
                s2 = s2_scr[:, lanes]
                _top_values(s1, v1_scr, lanes, PEER_TOPK)
                _top_values(s2, v2_scr, lanes, PEER_TOPK)
                v1 = v1_scr[:, lanes]
                v2 = v2_scr[:, lanes]
                rank2 = _rank_among(s2, v2, PEER_TOPK)
                off = 0
                for r1, cnt in enumerate(_CAND_COUNTS):
                    cand_scr[off:off + cnt, lanes] = v1[r1:r1 + 1] + v2[0:cnt]
                    off += cnt
                if _CAND_ROWS > _N_CAND:
                    cand_scr[_N_CAND:_CAND_ROWS, lanes] = jnp.full(
                        (_CAND_ROWS - _N_CAND, LANES), NEG_BIG, F32)
                cur = cand_scr[:, lanes]
                top = v1[0:1] + v2[0:1]
                z = jnp.zeros((1, LANES), F32)
                tau = top
                for _ in range(PEER_TOPK):
                    tau = jnp.max(cur, axis=0, keepdims=True)
                    z = z + jnp.exp(tau - top)
                    cur = jnp.where(cur == tau, NEG_BIG, cur)
                count = jnp.zeros((PEER_NKEYS, LANES), F32)
                for r in range(PEER_TOPK):
                    v1r = v1[r:r + 1]
                    n_r = jnp.sum(jnp.where(v1r + v2 >= tau, 1.0, 0.0), axis=0, keepdims=True)
                    count = jnp.where(s1 == v1r, n_r, count)
                count_pk = _pair_bf16(count)
                coef_pk = _pair_bf16(jnp.exp(s1 - v1[0:1]) / z)
                for c in range(PEER_NKEYS // CHUNK_KEYS):
                    keys = slice(c * CHUNK_KEYS, (c + 1) * CHUNK_KEYS)
                    pk_scr[hd, c, 0:CHUNK_KEYS, lanes] = count_pk[keys]
                    pk_scr[hd, c, CHUNK_KEYS:2 * CHUNK_KEYS, lanes] = coef_pk[keys]
                rb_scr[hd, :, lanes] = rank2.astype(BF16)
                e2_scr[hd, :, lanes] = jnp.exp(s2 - v2[0:1]).astype(BF16)
            return carry

        lax.fori_loop(0, PEER_HEADS, per_head, 0)

    def pre_activations(c):
        return jnp.dot(u_ref[c], h2t[...], preferred_element_type=F32)

    def gate(c, act_all):
        chunk = j * chunks_per_step + c
        for lt in range(n_lane_tiles):
            lanes = slice(lt * LANES, (lt + 1) * LANES)
            for part in range(PEER_NKEYS // GATE_ROWS):
                keys = slice(part * GATE_ROWS, (part + 1) * GATE_ROWS)
                ws = [jnp.zeros((GATE_ROWS, LANES), BF16) for _ in range(CHUNK_KEYS)]
                for hd in range(PEER_HEADS):
                    pk = pk_scr[hd, chunk, 0:2 * CHUNK_KEYS, lanes]
                    rank2 = rb_scr[hd, keys, lanes]
                    e2 = e2_scr[hd, keys, lanes]
                    for g in range(CHUNK_KEYS):
                        count = _bcast_rows_bf16(pk[g:g + 1])
                        coef = _bcast_rows_bf16(pk[CHUNK_KEYS + g:CHUNK_KEYS + g + 1])
                        ws[g] = ws[g] + jnp.where(rank2 < count, e2, jnp.zeros_like(e2)) * coef
                for g in range(CHUNK_KEYS):
                    first = g * PEER_NKEYS + part * GATE_ROWS
                    rows = slice(first, first + GATE_ROWS)
                    act = _gelu_tanh(act_all[rows, lanes])
                    p_scr[c % 2, rows, lanes] = (ws[g].astype(F32) * act).astype(BF16)

    def accumulate(c):
        acc_scr[...] += jnp.dot(vt_ref[c], p_scr[c % 2], preferred_element_type=F32)

    act_next = pre_activations(0)
    for c in range(chunks_per_step):
        act_cur = act_next
        if c + 1 < chunks_per_step:
            act_next = pre_activations(c + 1)
        gate(c, act_cur)
        accumulate(c)

    @pl.when(j == pl.num_programs(1) - 1)
    def _finish():
        peer = acc_scr[...].T
        mod = mod_ref[...]
        for b in range(nb):
            gate2 = mod[b, 5:6, :]
            out_ref[b] = xmid_ref[b] + gate2 * peer[b * ts:(b + 1) * ts, :]


def _peer(x_mid, h2, mod, w, *, nb, ts):
    n_rows, seq, _ = x_mid.shape
    tm = nb * ts
    tiles_per_row = seq // ts
    n_tiles = (n_rows // nb) * tiles_per_row
    chunk = CHUNK_KEYS * PEER_NKEYS
    n_chunks = PEER_N // chunk
    chunks_per_step = PEER_EXPERT_BLOCK // chunk
    n_steps = PEER_N // PEER_EXPERT_BLOCK

    tok = lambda i, j: (i // tiles_per_row, i % tiles_per_row, 0)
    row = lambda i, j: (i // tiles_per_row, 0, 0)
    in_specs = [
        pl.BlockSpec((nb, ts, D_MODEL), tok),
        pl.BlockSpec((nb, ts, D_MODEL), tok),
        pl.BlockSpec((nb, 6, D_MODEL), row),
        _const_spec((D_MODEL, PEER_HEADS * 2 * PEER_HALF)),
        _const_spec((PEER_HEADS, PEER_NKEYS, PEER_HALF)),
        _const_spec((PEER_HEADS, PEER_NKEYS, PEER_HALF)),
        pl.BlockSpec((chunks_per_step, chunk, D_MODEL), lambda i, j: (j, 0, 0)),
        pl.BlockSpec((chunks_per_step, D_MODEL, chunk), lambda i, j: (j, 0, 0)),
    ]
    per_head = (PEER_HEADS, PEER_NKEYS, tm)
    scratch = [
        pltpu.VMEM((tm, D_MODEL), BF16),
        pltpu.VMEM((D_MODEL, tm), BF16),
        pltpu.VMEM((PEER_NKEYS, tm), F32),
        pltpu.VMEM((PEER_NKEYS, tm), F32),
        pltpu.VMEM((PEER_HEADS, n_chunks, 8, tm), F32),
        pltpu.VMEM(per_head, BF16),
        pltpu.VMEM(per_head, BF16),
        pltpu.VMEM((PEER_TOPK, tm), F32),
        pltpu.VMEM((PEER_TOPK, tm), F32),
        pltpu.VMEM((_CAND_ROWS, tm), F32),
        pltpu.VMEM((2, chunk, tm), BF16),
        pltpu.VMEM((D_MODEL, tm), F32),
    ]
    return pl.pallas_call(
        functools.partial(_peer_kernel, nb=nb, ts=ts),
        grid=(n_tiles, n_steps),
        in_specs=in_specs,
        out_specs=pl.BlockSpec((nb, ts, D_MODEL), tok),
        out_shape=jax.ShapeDtypeStruct(x_mid.shape, F32),
        scratch_shapes=scratch,
        compiler_params=pltpu.CompilerParams(
            dimension_semantics=("parallel", "arbitrary"), vmem_limit_bytes=VMEM_LIMIT),
        name="peer",
    )(x_mid, h2, mod, w["w_pq"], w["sub_keys1"], w["sub_keys2"], w["expert_u"], w["expert_vt"])


def _layer_weights(l, norm1_gain, norm2_gain, w_in, q_gain, k_gain, rel_table, conv_w, w_o_att,
                   w_o_conv, w_o, w_pq, sub_keys1, sub_keys2, expert_u, expert_v):
    head_of = jnp.arange(ATT_DIM) // HEAD_DIM
    seg = jnp.where(head_of[:, None] == head_of[None, :], 1.0 / HEAD_DIM, 0.0).astype(BF16)
    return {
        "g1": norm1_gain[l].reshape(1, D_MODEL),
        "g2": norm2_gain[l].reshape(1, D_MODEL),
        "w_in": w_in[l].astype(BF16),
        "q_gain": jnp.tile(q_gain[l], ATT_HEADS).reshape(1, ATT_DIM),
        "k_gain": jnp.tile(k_gain[l], ATT_HEADS).reshape(1, ATT_DIM),
        "seg": seg,
        "rel_table": rel_table[l],
        "conv_w": conv_w[l],
        "w_o_att": w_o_att[l].astype(BF16),
        "w_o_conv": w_o_conv[l].astype(BF16),
        "w_o": w_o[l].astype(BF16),
        "w_pq": w_pq[l].astype(BF16),
        "sub_keys1": sub_keys1[l].astype(BF16),
        "sub_keys2": sub_keys2[l].astype(BF16),
        "expert_u": expert_u[l].astype(BF16).reshape(-1, CHUNK_KEYS * PEER_NKEYS, D_MODEL),
        "expert_vt": expert_v[l].astype(BF16).reshape(-1, CHUNK_KEYS * PEER_NKEYS, D_MODEL).transpose(0, 2, 1),
    }


def kernel(x_prompt, c_prompt, x_sample, c_sample, cache_k, cache_v, cache_conv, w_ada, b_ada, norm1_gain, norm2_gain, w_in, q_gain, k_gain, rel_table, conv_w, w_o_att, w_o_conv, w_o, w_pq, sub_keys1, sub_keys2, expert_u, expert_v):
    depth = w_ada.shape[0]
    nbp, seq_p, _ = x_prompt.shape
    nbs, seq_s, _ = x_sample.shape
    xp, xs = x_prompt, x_sample
    c_all = jnp.concatenate([c_prompt, c_sample], axis=0)
    outs = [[] for _ in range(6)]
    for l in range(depth):
        w = _layer_weights(l, norm1_gain, norm2_gain, w_in, q_gain, k_gain, rel_table, conv_w,
                           w_o_att, w_o_conv, w_o, w_pq, sub_keys1, sub_keys2, expert_u, expert_v)
        mod = _adaln(c_all, w_ada[l], b_ada[l]).reshape(nbp + nbs, 6, D_MODEL)
        mod_p, mod_s = mod[:nbp], mod[nbp:]

        zeros_ctx = jnp.zeros((nbp, LEFT_CTX, ATT_DIM), BF16)
        zeros_conv = jnp.zeros((nbp, CONV_WIDTH - 1, CONV_DIM), F32)
        xp_mid, h2p, kp, vp, cp = _mixer(xp, mod_p, zeros_ctx, zeros_ctx, zeros_conv, w,
                                         tm=MIXER_TILE, mask_pad=True)
        ck = cache_k[l].reshape(nbs, -1, ATT_DIM).astype(BF16)
        cv = cache_v[l].reshape(nbs, -1, ATT_DIM).astype(BF16)
        xs_mid, h2s, kn, vn, cn = _mixer(xs, mod_s, ck, cv, cache_conv[l], w,
                                         tm=seq_s, mask_pad=False)

        xp = _peer(xp_mid, h2p, mod_p, w, nb=1, ts=PEER_TILE)
        xs = _peer(xs_mid, h2s, mod_s, w, nb=PEER_TILE // seq_s, ts=seq_s)

        for lst, val in zip(outs, (kp, vp, cp, kn, vn, cn)):
            lst.append(val)
    kp, vp, cp, kn, vn, cn = (jnp.stack(v) for v in outs)
    heads = (ATT_HEADS, HEAD_DIM)
    return (xp, xs,
            kp.reshape(kp.shape[:3] + heads), vp.reshape(vp.shape[:3] + heads), cp,
            kn.reshape(kn.shape[:3] + heads), vn.reshape(vn.shape[:3] + heads), cn)
```

```python
import functools

import jax
import jax.numpy as jnp
import numpy as np
from jax import lax
from jax.experimental import pallas as pl
from jax.experimental.pallas import tpu as pltpu

D_MODEL = 1024
CHUNK = 64
LEFT_CTX = 512
ATT_HEADS = 8
HEAD_DIM = 64
ATT_DIM = ATT_HEADS * HEAD_DIM
REL_CLIP = 128
CONV_DIM = 512
CONV_WIDTH = 3
PEER_HEADS = 8
PEER_NKEYS = 128
PEER_N = PEER_NKEYS * PEER_NKEYS
PEER_HALF = 128
PEER_TOPK = 16
EPS = 1e-6
IN_COLS = 3 * ATT_DIM + 3 * CONV_DIM + 2 * D_MODEL

LANES = 128
MIXER_TILE = 256
PEER_TILE = 512
PEER_EXPERT_BLOCK = 2048
VMEM_LIMIT = 56 * 1024 * 1024
NEG_BIG = -3.0e38
MASKED = -1e30

F32 = jnp.float32
BF16 = jnp.bfloat16
_NT = (((1,), (1,)), ((), ()))


def _const_spec(shape):
    zeros = (0,) * len(shape)
    return pl.BlockSpec(shape, lambda *_: zeros, pipeline_mode=pl.Buffered(1))


def _adaln_kernel(c_ref, w_ref, b_ref, o_ref):
    c = c_ref[...]
    s = c * jax.nn.sigmoid(c)
    o_ref[...] = jnp.dot(s.astype(BF16), w_ref[...].astype(BF16),
                         preferred_element_type=F32) + b_ref[...]


def _adaln(c_all, w_ada, b_ada):
    n = c_all.shape[0]
    cols = w_ada.shape[1]
    blk = 1024
    return pl.pallas_call(
        _adaln_kernel,
        grid=(cols // blk,),
        in_specs=[pl.BlockSpec((n, D_MODEL), lambda i: (0, 0)),
                  pl.BlockSpec((D_MODEL, blk), lambda i: (0, i)),
                  pl.BlockSpec((1, blk), lambda i: (0, i))],
        out_specs=pl.BlockSpec((n, blk), lambda i: (0, i)),
        out_shape=jax.ShapeDtypeStruct((n, cols), F32),
        compiler_params=pltpu.CompilerParams(dimension_semantics=("parallel",)),
        name="adaln",
    )(c_all, w_ada, b_ada.reshape(1, cols))


def _rms(x, gain_row):
    ms = jnp.mean(x * x, axis=-1, keepdims=True)
    return x * lax.rsqrt(ms + EPS) * gain_row


def _head_rms(t, seg, gain_row):
    sq = t * t
    hi = sq.astype(BF16)
    lo = (sq - hi.astype(F32)).astype(BF16)
    ms = (jnp.dot(hi, seg, preferred_element_type=F32)
          + jnp.dot(lo, seg, preferred_element_type=F32))
    return t * lax.rsqrt(ms + EPS) * gain_row


def _mixer_kernel(x_ref, mod_ref, ctxk_ref, ctxv_ref, cbuf_ref, g1_ref, win_ref, qg_ref, kg_ref,
                  seg_ref, bias_ref, cw_ref, woa_ref, woc_ref, wo_ref, g2_ref,
                  xmid_ref, h2_ref, kout_ref, vout_ref, cstate_ref,
                  kbuf, vbuf, ubuf, att_scr, *, tm, mask_pad):
    j = pl.program_id(1)
    band = LEFT_CTX + tm

    @pl.when(j == 0)
    def _():
        kbuf[0:LEFT_CTX, :] = ctxk_ref[0]
        vbuf[0:LEFT_CTX, :] = ctxv_ref[0]
        ubuf[0:8, :] = jnp.zeros((8, CONV_DIM), F32)
        ubuf[6:8, :] = cbuf_ref[0]

    x = x_ref[0]
    mod = mod_ref[0]
    shift1, scale1, gate1 = mod[0:1], mod[1:2], mod[2:3]
    shift2, scale2 = mod[3:4], mod[4:5]

    h = _rms(x, g1_ref[...]) * (1.0 + scale1) + shift1
    hb = h.astype(BF16)

    zq = jnp.dot(hb, win_ref[:, 0:3 * ATT_DIM], preferred_element_type=F32)
    seg = seg_ref[...]
    qn = _head_rms(zq[:, 0:ATT_DIM], seg, qg_ref[...])
    kn = _head_rms(zq[:, ATT_DIM:2 * ATT_DIM], seg, kg_ref[...])
    vv = zq[:, 2 * ATT_DIM:3 * ATT_DIM]
    kout_ref[0] = kn
    vout_ref[0] = vv
    kbuf[LEFT_CTX:band, :] = kn.astype(BF16)
    vbuf[LEFT_CTX:band, :] = vv.astype(BF16)
    qs = (qn * (HEAD_DIM ** -0.5)).astype(BF16)

    if mask_pad:
        col = lax.broadcasted_iota(jnp.int32, (tm, band), 1)
        valid = col >= (LEFT_CTX - j * tm)
    for hd in range(ATT_HEADS):
        hs = slice(hd * HEAD_DIM, (hd + 1) * HEAD_DIM)
        s = lax.dot_general(qs[:, hs], kbuf[:, hs], _NT, preferred_element_type=F32)
        s = s + bias_ref[hd]
        if mask_pad:
            s = jnp.where(valid, s, MASKED)
        m = jnp.max(s, axis=-1, keepdims=True)
        p = jnp.exp(s - m)
        l = jnp.sum(p, axis=-1, keepdims=True)
        o = jnp.dot(p.astype(BF16), vbuf[:, hs], preferred_element_type=F32)
        att_scr[:, hs] = o / l
    y_att = jnp.dot(att_scr[...].astype(BF16), woa_ref[...], preferred_element_type=F32)

    kbuf[0:LEFT_CTX, :] = kbuf[tm:tm + LEFT_CTX, :]
    vbuf[0:LEFT_CTX, :] = vbuf[tm:tm + LEFT_CTX, :]

    zc = jnp.dot(hb, win_ref[:, 3 * ATT_DIM:3 * ATT_DIM + 3 * CONV_DIM], preferred_element_type=F32)
    cb = zc[:, 0:CONV_DIM]
    ubuf[8:8 + tm, :] = zc[:, CONV_DIM:2 * CONV_DIM] * zc[:, 2 * CONV_DIM:3 * CONV_DIM]
    cw = cw_ref[...]
    y = ubuf[6:6 + tm, :] * cw[0:1]
    y = y + ubuf[7:7 + tm, :] * cw[1:2]
    y = y + ubuf[8:8 + tm, :] * cw[2:3]
    tail = ubuf[6 + tm:8 + tm, :]
    cstate_ref[0] = tail
    ubuf[6:8, :] = tail
    y_conv = jnp.dot((cb * y).astype(BF16), woc_ref[...], preferred_element_type=F32)

    zg = jnp.dot(hb, win_ref[:, 3 * ATT_DIM + 3 * CONV_DIM:IN_COLS], preferred_element_type=F32)
    merged = (jax.nn.sigmoid(zg[:, 0:D_MODEL]) * y_att
              + jax.nn.sigmoid(zg[:, D_MODEL:2 * D_MODEL]) * y_conv)
    mixed = jnp.dot(merged.astype(BF16), wo_ref[...], preferred_element_type=F32)
    x1 = x + gate1 * mixed
    xmid_ref[0] = x1
    h2 = _rms(x1, g2_ref[...]) * (1.0 + scale2) + shift2
    h2_ref[0] = h2.astype(BF16)


def _rel_bias(rel_table, tm):
    band = LEFT_CTX + tm
    period = band + tm - 1
    n_far_past = LEFT_CTX - REL_CLIP
    n_future = max(band - n_far_past - (2 * REL_CLIP + 1), 0)
    nh = rel_table.shape[0]
    by_offset = jnp.concatenate([
        jnp.concatenate([
            jnp.broadcast_to(rel_table[:, -1:], (nh, n_far_past)),
            rel_table[:, ::-1],
            jnp.broadcast_to(rel_table[:, :1], (nh, n_future))], axis=1)[:, :band],
        jnp.broadcast_to(rel_table[:, -1:], (nh, tm - 1))], axis=1)
    skew = jnp.tile(by_offset, (1, tm))[:, :tm * (period - 1)].reshape(nh, tm, period - 1)
    bias = skew[:, :, :band]
    qi = np.arange(tm)[:, None]
    kj = np.arange(band)[None, :]
    lo = (qi // CHUNK) * CHUNK
    allowed = (kj >= lo) & (kj < lo + LEFT_CTX + CHUNK)
    return jnp.where(allowed[None], bias, MASKED).astype(F32)


def _mixer(x, mod, ctx_k, ctx_v, conv_buf, w, *, tm, mask_pad):
    nb, seq, _ = x.shape
    n_steps = seq // tm
    n_keep = min(LEFT_CTX, seq)
    first_kept = n_steps - n_keep // tm
    band = LEFT_CTX + tm
    bias = _rel_bias(w["rel_table"], tm)

    tile = lambda b, j: (b, j, 0)
    per_row = lambda b, j: (b, 0, 0)
    kept = lambda b, j: (b, jnp.maximum(j - first_kept, 0), 0)
    in_specs = [
        pl.BlockSpec((1, tm, D_MODEL), tile),
        pl.BlockSpec((1, 6, D_MODEL), per_row),
        pl.BlockSpec((1, LEFT_CTX, ATT_DIM), per_row),
        pl.BlockSpec((1, LEFT_CTX, ATT_DIM), per_row),
        pl.BlockSpec((1, CONV_WIDTH - 1, CONV_DIM), per_row),
        _const_spec((1, D_MODEL)),
        _const_spec((D_MODEL, IN_COLS)),
        _const_spec((1, ATT_DIM)),
        _const_spec((1, ATT_DIM)),
        _const_spec((ATT_DIM, ATT_DIM)),
        _const_spec((ATT_HEADS, tm, band)),
        _const_spec((CONV_WIDTH, CONV_DIM)),
        _const_spec((ATT_DIM, D_MODEL)),
        _const_spec((CONV_DIM, D_MODEL)),
        _const_spec((D_MODEL, D_MODEL)),
        _const_spec((1, D_MODEL)),
    ]
    out_specs = [
        pl.BlockSpec((1, tm, D_MODEL), tile),
        pl.BlockSpec((1, tm, D_MODEL), tile),
        pl.BlockSpec((1, tm, ATT_DIM), kept),
        pl.BlockSpec((1, tm, ATT_DIM), kept),
        pl.BlockSpec((1, CONV_WIDTH - 1, CONV_DIM), per_row),
    ]
    out_shape = [
        jax.ShapeDtypeStruct((nb, seq, D_MODEL), F32),
        jax.ShapeDtypeStruct((nb, seq, D_MODEL), BF16),
        jax.ShapeDtypeStruct((nb, n_keep, ATT_DIM), F32),
        jax.ShapeDtypeStruct((nb, n_keep, ATT_DIM), F32),
        jax.ShapeDtypeStruct((nb, CONV_WIDTH - 1, CONV_DIM), F32),
    ]
    scratch = [
        pltpu.VMEM((band, ATT_DIM), BF16),
        pltpu.VMEM((band, ATT_DIM), BF16),
        pltpu.VMEM((8 + tm, CONV_DIM), F32),
        pltpu.VMEM((tm, ATT_DIM), F32),
    ]
    return pl.pallas_call(
        functools.partial(_mixer_kernel, tm=tm, mask_pad=mask_pad),
        grid=(nb, n_steps),
        in_specs=in_specs,
        out_specs=out_specs,
        out_shape=out_shape,
        scratch_shapes=scratch,
        compiler_params=pltpu.CompilerParams(
            dimension_semantics=("parallel", "arbitrary"), vmem_limit_bytes=VMEM_LIMIT),
        name="mixer",
    )(x, mod, ctx_k, ctx_v, conv_buf, w["g1"], w["w_in"], w["q_gain"], w["k_gain"], w["seg"],
      bias, w["conv_w"], w["w_o_att"], w["w_o_conv"], w["w_o"], w["g2"])


def _gelu_tanh(x):
    k0 = -2.0 * 1.4426950408889634 * 0.7978845608028654
    k1 = k0 * 0.044715
    return x / (1.0 + jnp.exp2(x * (k0 + k1 * (x * x))))


def _sorting_network(n):
    pairs = []
    p = 1
    while p < n:
        k = p
        while k >= 1:
            for j in range(k % p, n - k, 2 * k):
                for i in range(min(k, n - j - k)):
                    if (i + j) // (2 * p) == (i + j + k) // (2 * p):
                        pairs.append((i + j, i + j + k))
            k //= 2
        p *= 2
    return tuple(pairs)


SUBLANES = 8


def _top_rows(cur, n):
    groups = cur.shape[0] // SUBLANES
    lists = [cur[k * SUBLANES:(k + 1) * SUBLANES] for k in range(groups)]
    for a, b in _sorting_network(groups):
        lists[a], lists[b] = jnp.maximum(lists[a], lists[b]), jnp.minimum(lists[a], lists[b])
    lists = lists + [jnp.full((SUBLANES, LANES), NEG_BIG, F32)] * (n - groups)
    shift = SUBLANES // 2
    while shift >= 1:
        partner = [pltpu.roll(x, shift, 0) for x in lists]
        lists = [jnp.maximum(lists[k], partner[n - 1 - k]) for k in range(n)]
        d = n // 2
        while d >= 1:
            for i in range(n):
                if i & d == 0:
                    lists[i], lists[i + d] = (jnp.maximum(lists[i], lists[i + d]),
                                              jnp.minimum(lists[i], lists[i + d]))
            d //= 2
        shift //= 2
    return lists


def _top_values(cur, dst_ref, lanes, n):
    for r, row in enumerate(_top_rows(cur, n)):
        dst_ref[r:r + 1, lanes] = row[0:1]


def _rank_among(x, tops, n):
    rank = jnp.full(x.shape, float(n), F32)
    for r in range(n - 1, -1, -1):
        rank = jnp.where(x >= tops[r:r + 1], float(r), rank)
    return rank


_CAND_COUNTS = tuple(PEER_TOPK // (r1 + 1) for r1 in range(PEER_TOPK))
_N_CAND = sum(_CAND_COUNTS)
_CAND_ROWS = SUBLANES * pl.next_power_of_2(-(-_N_CAND // SUBLANES))


def _pair_bf16(x):
    hi = pltpu.bitcast(x.astype(BF16).astype(F32), jnp.uint32)
    return pltpu.bitcast(hi | lax.shift_right_logical(hi, jnp.uint32(16)), F32)


GATE_ROWS = 64
CHUNK_KEYS = 4


def _bcast_rows_bf16(pair_row):
    return pltpu.bitcast(jnp.broadcast_to(pair_row, (GATE_ROWS // 2, LANES)), BF16)


def _peer_kernel(xmid_ref, h2_ref, mod_ref, wpq_ref, k1_ref, k2_ref, u_ref, vt_ref, out_ref,
                 h2s, h2t, s1_scr, s2_scr, pk_scr, rb_scr, e2_scr, v1_scr, v2_scr, cand_scr,
                 p_scr, acc_scr, *, nb, ts):
    j = pl.program_id(1)
    tm = nb * ts
    n_lane_tiles = tm // LANES
    chunks_per_step = u_ref.shape[0]

    @pl.when(j == 0)
    def _retrieve():
        for b in range(nb):
            h2s[b * ts:(b + 1) * ts, :] = h2_ref[b]
        acc_scr[...] = jnp.zeros(acc_scr.shape, F32)
        h2v = h2s[...]
        h2t[...] = h2v.astype(F32).T.astype(BF16)

        def per_head(hd, carry):
            col = pl.multiple_of(hd * (2 * PEER_HALF), 2 * PEER_HALF)
            q = jnp.dot(h2v, wpq_ref[:, pl.ds(col, 2 * PEER_HALF)], preferred_element_type=F32)
            q1 = q[:, 0:PEER_HALF].astype(BF16)
            q2 = q[:, PEER_HALF:2 * PEER_HALF].astype(BF16)
            s1_scr[...] = lax.dot_general(k1_ref[hd], q1, _NT, preferred_element_type=F32)
            s2_scr[...] = lax.dot_general(k2_ref[hd], q2, _NT, preferred_element_type=F32)
            for lt in range(n_lane_tiles):
                lanes = slice(lt * LANES, (lt + 1) * LANES)
                s1 = s1_scr[:, lanes]
                s2 = s2_scr[:, lanes]
                _top_values(s1, v1_scr, lanes, PEER_TOPK)
                _top_values(s2, v2_scr, lanes, PEER_TOPK)
                v1 = v1_scr[:, lanes]
                v2 = v2_scr[:, lanes]
                rank2 = _rank_among(s2, v2, PEER_TOPK)
                off = 0
                for r1, cnt in enumerate(_CAND_COUNTS):
                    cand_scr[off:off + cnt, lanes] = v1[r1:r1 + 1] + v2[0:cnt]
                    off += cnt
                if _CAND_ROWS > _N_CAND:
                    cand_scr[_N_CAND:_CAND_ROWS, lanes] = jnp.full(
                        (_CAND_ROWS - _N_CAND, LANES), NEG_BIG, F32)
                joint = _top_rows(cand_scr[:, lanes], PEER_TOPK)
                top = joint[0][0:1]
                tau = joint[PEER_TOPK - 1][0:1]
                z = jnp.zeros((1, LANES), F32)
                for row in joint:
                    z = z + jnp.exp(row[0:1] - top)
                count = jnp.zeros((PEER_NKEYS, LANES), F32)
                for r in range(PEER_TOPK):
                    v1r = v1[r:r + 1]
                    n_r = jnp.sum(jnp.where(v1r + v2 >= tau, 1.0, 0.0), axis=0, keepdims=True)
                    count = jnp.where(s1 == v1r, n_r, count)
                count_pk = _pair_bf16(count)
                coef_pk = _pair_bf16(jnp.exp(s1 - v1[0:1]) / z)
                for c in range(PEER_NKEYS // CHUNK_KEYS):
                    keys = slice(c * CHUNK_KEYS, (c + 1) * CHUNK_KEYS)
                    pk_scr[hd, c, 0:CHUNK_KEYS, lanes] = count_pk[keys]
                    pk_scr[hd, c, CHUNK_KEYS:2 * CHUNK_KEYS, lanes] = coef_pk[keys]
                rb_scr[hd, :, lanes] = rank2.astype(BF16)
                e2_scr[hd, :, lanes] = jnp.exp(s2 - v2[0:1]).astype(BF16)
            return carry

        lax.fori_loop(0, PEER_HEADS, per_head, 0)

    def pre_activations(c):
        return jnp.dot(u_ref[c], h2t[...], preferred_element_type=F32)

    def gate(c, act_all):
        chunk = j * chunks_per_step + c
        for lt in range(n_lane_tiles):
            lanes = slice(lt * LANES, (lt + 1) * LANES)
            for part in range(PEER_NKEYS // GATE_ROWS):
                keys = slice(part * GATE_ROWS, (part + 1) * GATE_ROWS)
                ws = [jnp.zeros((GATE_ROWS, LANES), BF16) for _ in range(CHUNK_KEYS)]
                for hd in range(PEER_HEADS):
                    pk = pk_scr[hd, chunk, 0:2 * CHUNK_KEYS, lanes]
                    rank2 = rb_scr[hd, keys, lanes]
                    e2 = e2_scr[hd, keys, lanes]
                    for g in range(CHUNK_KEYS):
                        count = _bcast_rows_bf16(pk[g:g + 1])
                        coef = _bcast_rows_bf16(pk[CHUNK_KEYS + g:CHUNK_KEYS + g + 1])
                        ws[g] = ws[g] + jnp.where(rank2 < count, e2, jnp.zeros_like(e2)) * coef
                for g in range(CHUNK_KEYS):
                    first = g * PEER_NKEYS + part * GATE_ROWS
                    rows = slice(first, first + GATE_ROWS)
                    act = _gelu_tanh(act_all[rows, lanes])
                    p_scr[c % 2, rows, lanes] = (ws[g].astype(F32) * act).astype(BF16)

    def accumulate(c):
        acc_scr[...] += jnp.dot(vt_ref[c], p_scr[c % 2], preferred_element_type=F32)

    act_next = pre_activations(0)
    for c in range(chunks_per_step):
        act_cur = act_next
        if c + 1 < chunks_per_step:
            act_next = pre_activations(c + 1)
        gate(c, act_cur)
        accumulate(c)

    @pl.when(j == pl.num_programs(1) - 1)
    def _finish():
        peer = acc_scr[...].T
        mod = mod_ref[...]
        for b in range(nb):
            gate2 = mod[b, 5:6, :]
            out_ref[b] = xmid_ref[b] + gate2 * peer[b * ts:(b + 1) * ts, :]


def _peer(x_mid, h2, mod, w, *, nb, ts):
    n_rows, seq, _ = x_mid.shape
    tm = nb * ts
    tiles_per_row = seq // ts
    n_tiles = (n_rows // nb) * tiles_per_row
    chunk = CHUNK_KEYS * PEER_NKEYS
    n_chunks = PEER_N // chunk
    chunks_per_step = PEER_EXPERT_BLOCK // chunk
    n_steps = PEER_N // PEER_EXPERT_BLOCK

    tok = lambda i, j: (i // tiles_per_row, i % tiles_per_row, 0)
    row = lambda i, j: (i // tiles_per_row, 0, 0)
    in_specs = [
        pl.BlockSpec((nb, ts, D_MODEL), tok),
        pl.BlockSpec((nb, ts, D_MODEL), tok),
        pl.BlockSpec((nb, 6, D_MODEL), row),
        _const_spec((D_MODEL, PEER_HEADS * 2 * PEER_HALF)),
        _const_spec((PEER_HEADS, PEER_NKEYS, PEER_HALF)),
        _const_spec((PEER_HEADS, PEER_NKEYS, PEER_HALF)),
        pl.BlockSpec((chunks_per_step, chunk, D_MODEL), lambda i, j: (j, 0, 0)),
        pl.BlockSpec((chunks_per_step, D_MODEL, chunk), lambda i, j: (j, 0, 0)),
    ]
    per_head = (PEER_HEADS, PEER_NKEYS, tm)
    scratch = [
        pltpu.VMEM((tm, D_MODEL), BF16),
        pltpu.VMEM((D_MODEL, tm), BF16),
        pltpu.VMEM((PEER_NKEYS, tm), F32),
        pltpu.VMEM((PEER_NKEYS, tm), F32),
        pltpu.VMEM((PEER_HEADS, n_chunks, 8, tm), F32),
        pltpu.VMEM(per_head, BF16),
        pltpu.VMEM(per_head, BF16),
        pltpu.VMEM((PEER_TOPK, tm), F32),
        pltpu.VMEM((PEER_TOPK, tm), F32),
        pltpu.VMEM((_CAND_ROWS, tm), F32),
        pltpu.VMEM((2, chunk, tm), BF16),
        pltpu.VMEM((D_MODEL, tm), F32),
    ]
    return pl.pallas_call(
        functools.partial(_peer_kernel, nb=nb, ts=ts),
        grid=(n_tiles, n_steps),
        in_specs=in_specs,
        out_specs=pl.BlockSpec((nb, ts, D_MODEL), tok),
        out_shape=jax.ShapeDtypeStruct(x_mid.shape, F32),
        scratch_shapes=scratch,
        compiler_params=pltpu.CompilerParams(
            dimension_semantics=("parallel", "arbitrary"), vmem_limit_bytes=VMEM_LIMIT),
        name="peer",
    )(x_mid, h2, mod, w["w_pq"], w["sub_keys1"], w["sub_keys2"], w["expert_u"], w["expert_vt"])


def _layer_weights(l, norm1_gain, norm2_gain, w_in, q_gain, k_gain, rel_table, conv_w, w_o_att,
                   w_o_conv, w_o, w_pq, sub_keys1, sub_keys2, expert_u, expert_v):
    head_of = jnp.arange(ATT_DIM) // HEAD_DIM
    seg = jnp.where(head_of[:, None] == head_of[None, :], 1.0 / HEAD_DIM, 0.0).astype(BF16)
    return {
        "g1": norm1_gain[l].reshape(1, D_MODEL),
        "g2": norm2_gain[l].reshape(1, D_MODEL),
        "w_in": w_in[l].astype(BF16),
        "q_gain": jnp.tile(q_gain[l], ATT_HEADS).reshape(1, ATT_DIM),
        "k_gain": jnp.tile(k_gain[l], ATT_HEADS).reshape(1, ATT_DIM),
        "seg": seg,
        "rel_table": rel_table[l],
        "conv_w": conv_w[l],
        "w_o_att": w_o_att[l].astype(BF16),
        "w_o_conv": w_o_conv[l].astype(BF16),
        "w_o": w_o[l].astype(BF16),
        "w_pq": w_pq[l].astype(BF16),
        "sub_keys1": sub_keys1[l].astype(BF16),
        "sub_keys2": sub_keys2[l].astype(BF16),
        "expert_u": expert_u[l].astype(BF16).reshape(-1, CHUNK_KEYS * PEER_NKEYS, D_MODEL),
        "expert_vt": expert_v[l].astype(BF16).reshape(-1, CHUNK_KEYS * PEER_NKEYS, D_MODEL).transpose(0, 2, 1),
    }


def kernel(x_prompt, c_prompt, x_sample, c_sample, cache_k, cache_v, cache_conv, w_ada, b_ada, norm1_gain, norm2_gain, w_in, q_gain, k_gain, rel_table, conv_w, w_o_att, w_o_conv, w_o, w_pq, sub_keys1, sub_keys2, expert_u, expert_v):
    depth = w_ada.shape[0]
    nbp, seq_p, _ = x_prompt.shape
    nbs, seq_s, _ = x_sample.shape
    xp, xs = x_prompt, x_sample
    c_all = jnp.concatenate([c_prompt, c_sample], axis=0)
    outs = [[] for _ in range(6)]
    for l in range(depth):
        w = _layer_weights(l, norm1_gain, norm2_gain, w_in, q_gain, k_gain, rel_table, conv_w,
                           w_o_att, w_o_conv, w_o, w_pq, sub_keys1, sub_keys2, expert_u, expert_v)
        mod = _adaln(c_all, w_ada[l], b_ada[l]).reshape(nbp + nbs, 6, D_MODEL)
        mod_p, mod_s = mod[:nbp], mod[nbp:]

        zeros_ctx = jnp.zeros((nbp, LEFT_CTX, ATT_DIM), BF16)
        zeros_conv = jnp.zeros((nbp, CONV_WIDTH - 1, CONV_DIM), F32)
        xp_mid, h2p, kp, vp, cp = _mixer(xp, mod_p, zeros_ctx, zeros_ctx, zeros_conv, w,
                                         tm=MIXER_TILE, mask_pad=True)
        ck = cache_k[l].reshape(nbs, -1, ATT_DIM).astype(BF16)
        cv = cache_v[l].reshape(nbs, -1, ATT_DIM).astype(BF16)
        xs_mid, h2s, kn, vn, cn = _mixer(xs, mod_s, ck, cv, cache_conv[l], w,
                                         tm=seq_s, mask_pad=False)

        xp = _peer(xp_mid, h2p, mod_p, w, nb=1, ts=PEER_TILE)
        xs = _peer(xs_mid, h2s, mod_s, w, nb=PEER_TILE // seq_s, ts=seq_s)

        for lst, val in zip(outs, (kp, vp, cp, kn, vn, cn)):
            lst.append(val)
    kp, vp, cp, kn, vn, cn = (jnp.stack(v) for v in outs)
    heads = (ATT_HEADS, HEAD_DIM)
    return (xp, xs,
            kp.reshape(kp.shape[:3] + heads), vp.reshape(vp.shape[:3] + heads), cp,
            kn.reshape(kn.shape[:3] + heads), vn.reshape(vn.shape[:3] + heads), cn)
```
